```python
import jax, jax.numpy as jnp
from jax import lax
import numpy as np

D_MODEL = 4096
BATCH = 4
SEQ = 2048
DEPTH = 2
DEC_BATCH = 128
DEC_SEQ = 1
PAST_LEN = 16384
PAGE_SIZE = 128

N_MLA = (DEPTH + 1) // 2
N_HGRN = DEPTH // 2
MLA_HEADS = 32
Q_LORA = 1024
KV_LORA = 512
NOPE_DIM = 128
ROPE_DIM = 64
V_DIM = 128
ROPE_THETA = 10000.0
ATTN_SCALE = (NOPE_DIM + ROPE_DIM) ** -0.5
Q_BLOCK = 128
HG_KDIM = 128
HG_HEADS = D_MODEL // HG_KDIM
HG_VDIM = D_MODEL // HG_HEADS
HG_CHUNK = 64
D_FF = 11008
N_EXPERTS = 8
TOP_K = 2
D_EXPERT = 14336
N_ADA = 6
EPS = 1e-6
NEG_INF = -1e30

kernel_name = "hybrid_mla_hgrn2_adaln_decoder_step"

F32 = jnp.float32


def rms_norm(x, g):
    xf = x.astype(F32)
    y = xf * lax.rsqrt(jnp.mean(xf * xf, axis=-1, keepdims=True) + EPS)
    return (y * g.astype(F32)).astype(x.dtype)


def ada_modulation(c, w, b):
    m = jax.nn.silu(c) @ w + b
    return jnp.split(m[:, None, :], N_ADA, axis=-1)


def modulate(h, shift, scale):
    return h * (1.0 + scale) + shift


def rope_tables(pos):
    inv = ROPE_THETA ** (-jnp.arange(0, ROPE_DIM, 2, dtype=F32) / ROPE_DIM)
    ang = pos.astype(F32)[:, None] * inv[None, :]
    return jnp.cos(ang), jnp.sin(ang)


def apply_rope(x, cos, sin):
    half = x.shape[-1] // 2
    x1, x2 = x[..., :half], x[..., half:]
    cos, sin = cos.astype(x.dtype), sin.astype(x.dtype)
    return jnp.concatenate([x1 * cos - x2 * sin, x1 * sin + x2 * cos], axis=-1)


def mla_project(h, w_in, q_norm, w_uq, kv_norm, w_uk, cos, sin):
    b, t, _ = h.shape
    proj = h @ w_in
    cq = proj[..., :Q_LORA]
    ckv = rms_norm(proj[..., Q_LORA:Q_LORA + KV_LORA], kv_norm)
    kpe = apply_rope(proj[..., Q_LORA + KV_LORA:], cos, sin)
    q = (rms_norm(cq, q_norm) @ w_uq).reshape(b, t, MLA_HEADS, NOPE_DIM + ROPE_DIM)
    q_nope = q[..., :NOPE_DIM]
    q_pe = apply_rope(q[..., NOPE_DIM:], cos[:, None, :], sin[:, None, :])
    q_lat = jnp.einsum('bthn,lhn->bthl', q_nope, w_uk)
    return q_lat, q_pe, ckv, kpe


def latent_scores(q_lat, q_pe, ckv, kpe):
    s = jnp.einsum('bqhl,bkl->bhqk', q_lat, ckv, preferred_element_type=F32)
    s = s + jnp.einsum('bqhr,bkr->bhqk', q_pe, kpe, preferred_element_type=F32)
    return s * ATTN_SCALE


def mla_prompt_attend(q_lat, q_pe, ckv, kpe):
    b, t, h, l = q_lat.shape
    nb = t // Q_BLOCK
    ql = q_lat.reshape(b, nb, Q_BLOCK, h, l).transpose(1, 0, 2, 3, 4)
    qp = q_pe.reshape(b, nb, Q_BLOCK, h, ROPE_DIM).transpose(1, 0, 2, 3, 4)
    key_pos = jnp.arange(t)

    def block(args):
        ql_b, qp_b, start = args
        s = latent_scores(ql_b, qp_b, ckv, kpe)
        q_pos = start + jnp.arange(Q_BLOCK)
        s = jnp.where(key_pos[None, :] <= q_pos[:, None], s, NEG_INF)
        p = jax.nn.softmax(s, axis=-1).astype(ckv.dtype)
        return jnp.einsum('bhqk,bkl->bqhl', p, ckv)

    out = lax.map(block, (ql, qp, jnp.arange(nb) * Q_BLOCK))
    return out.transpose(1, 0, 2, 3, 4).reshape(b, t, h, l)


def mla_sample_attend(q_lat, q_pe, ckv_new, kpe_new, ckv_past, kpe_past):
    t = q_lat.shape[1]
    n_past = ckv_past.shape[1]
    s_past = latent_scores(q_lat, q_pe, ckv_past, kpe_past)
    s_new = latent_scores(q_lat, q_pe, ckv_new, kpe_new)
    s_new = jnp.where(jnp.tril(jnp.ones((t, t), bool)), s_new, NEG_INF)
    p = jax.nn.softmax(jnp.concatenate([s_past, s_new], axis=-1), axis=-1).astype(ckv_new.dtype)
    return (jnp.einsum('bhqk,bkl->bqhl', p[..., :n_past], ckv_past)
            + jnp.einsum('bhqk,bkl->bqhl', p[..., n_past:], ckv_new))


def mla_output(lat, w_uv, w_o):
    b, t = lat.shape[:2]
    o = jnp.einsum('bthl,lhv->bthv', lat, w_uv).reshape(b, t, MLA_HEADS * V_DIM)
    return o @ w_o


def hgrn_lower_bounds(raw):
    p = jax.nn.softmax(raw.astype(F32), axis=0)
    return jnp.cumsum(p, axis=0) - p[0:1]


def gated_linear_recurrence(q, k, v, log_f, s0):
    b, t, h, dk = q.shape
    dv = v.shape[-1]
    c = HG_CHUNK if t >= HG_CHUNK else t
    pad = (-t) % c
    n = (t + pad) // c

    def to_chunks(a):
        a = jnp.pad(a, ((0, 0), (0, pad), (0, 0), (0, 0)))
        return a.reshape(b, n, c, h, a.shape[-1]).transpose(1, 0, 3, 2, 4)

    causal = jnp.tril(jnp.ones((c, c), bool))
    mid = (c - 1) // 2

    def chunk_step(state, xs):
        qc, kc, vc, lc = xs
        cum = jnp.cumsum(lc, axis=2)
        ref = cum[:, :, mid:mid + 1]
        last = cum[:, :, -1:]
        a = jnp.einsum('bhik,bhjk->bhij', qc * jnp.exp(cum - ref), kc * jnp.exp(ref - cum))
        a = jnp.where(causal, a, 0.0)
        o = (jnp.einsum('bhik,bhkv->bhiv', qc * jnp.exp(cum), state)
             + jnp.einsum('bhij,bhjv->bhiv', a, vc))
        new_state = (jnp.exp(last[:, :, 0, :])[..., None] * state
                     + jnp.einsum('bhjk,bhjv->bhkv', kc * jnp.exp(last - cum), vc))
        return new_state, o

    s_final, o = lax.scan(chunk_step, s0, tuple(to_chunks(a) for a in (q, k, v, log_f)))
    o = o.transpose(1, 0, 3, 2, 4).reshape(b, n * c, h, dv)[:, :t]
    return o, s_final


def hgrn2_mixer(h, s0, w_in, lb, g_norm, w_o):
    b, t, d = h.shape
    q, f, i, g = jnp.split(h @ w_in, 4, axis=-1)
    shp = (b, t, HG_HEADS, HG_KDIM)
    lb = lb.reshape(HG_HEADS, HG_KDIM)
    log_f = jnp.logaddexp(jnp.log(lb), jnp.log1p(-lb) + jax.nn.log_sigmoid(f.astype(F32).reshape(shp)))
    k = -jnp.expm1(log_f)
    q = jax.nn.silu(q.astype(F32)).reshape(shp)
    v = i.astype(F32).reshape(b, t, HG_HEADS, HG_VDIM)
    o, s_new = gated_linear_recurrence(q, k, v, log_f, s0.astype(F32))
    o = rms_norm(o.reshape(b, t, d), g_norm) * jax.nn.silu(g.astype(F32))
    return o.astype(h.dtype) @ w_o, s_new


def swiglu(h, w1, w3, w2):
    return (jax.nn.silu(h @ w1) * (h @ w3)) @ w2


def moe_swiglu(h, router, w1, w3, w2):
    b, t, d = h.shape
    x = h.reshape(b * t, d)
    logits = jnp.einsum('nd,de->ne', x, router, preferred_element_type=F32)
    top_v, top_i = lax.top_k(logits, TOP_K)
    wts = jax.nn.softmax(top_v, axis=-1)
    combine = jnp.sum(jax.nn.one_hot(top_i, N_EXPERTS, dtype=F32) * wts[..., None], axis=1).astype(x.dtype)
    y = jnp.zeros_like(x)
    for e in range(N_EXPERTS):
        y = y + combine[:, e:e + 1] * swiglu(x, w1[e], w3[e], w2[e])
    return y.reshape(b, t, d)


def setup_inputs(seed: int = 0) -> dict:
    key = jax.random.key(seed)
    ks = iter(jax.random.split(key, 40))

    def nrm(shape, scale):
        return jax.random.normal(next(ks), shape, F32) * scale

    def gain(shape):
        return 1.0 + 0.02 * jax.random.normal(next(ks), shape, F32)

    n_pages = PAST_LEN // PAGE_SIZE
    n_pool = (DEC_BATCH * n_pages * 5) // 4
    page_table = jax.random.permutation(next(ks), n_pool)[:DEC_BATCH * n_pages]
    page_table = page_table.reshape(DEC_BATCH, n_pages).astype(jnp.int32)
    d = D_MODEL
    return {
        "x_prompt": nrm((BATCH, SEQ, d), 1.0),
        "x_sample": nrm((DEC_BATCH, DEC_SEQ, d), 1.0),
        "cache_ckv": nrm((N_MLA, n_pool, PAGE_SIZE, KV_LORA), 1.0),
        "cache_kpe": nrm((N_MLA, n_pool, PAGE_SIZE, ROPE_DIM), 1.0),
        "state_hgrn": nrm((N_HGRN, DEC_BATCH, HG_HEADS, HG_KDIM, HG_VDIM), 0.3),
        "page_table": page_table,
        "c_prompt": nrm((BATCH, d), 1.0),
        "c_sample": nrm((DEC_BATCH, d), 1.0),
        "ada_w": nrm((DEPTH, d, N_ADA * d), 0.5 * d ** -0.5),
        "ada_b": nrm((DEPTH, N_ADA * d), 0.02),
        "norm_g": gain((DEPTH, 2, d)),
        "mla_w_in": nrm((N_MLA, d, Q_LORA + KV_LORA + ROPE_DIM), d ** -0.5),
        "mla_q_norm": gain((N_MLA, Q_LORA)),
        "mla_w_uq": nrm((N_MLA, Q_LORA, MLA_HEADS * (NOPE_DIM + ROPE_DIM)), Q_LORA ** -0.5),
        "mla_kv_norm": gain((N_MLA, KV_LORA)),
        "mla_w_uk": nrm((N_MLA, KV_LORA, MLA_HEADS, NOPE_DIM), KV_LORA ** -0.5),
        "mla_w_uv": nrm((N_MLA, KV_LORA, MLA_HEADS, V_DIM), KV_LORA ** -0.5),
        "mla_w_o": nrm((N_MLA, MLA_HEADS * V_DIM, d), (MLA_HEADS * V_DIM) ** -0.5),
        "hg_w_in": nrm((N_HGRN, d, 4 * d), d ** -0.5),
        "hg_lower_bounds": nrm((DEPTH, d), 0.5),
        "hg_g_norm": gain((N_HGRN, d)),
        "hg_w_o": nrm((N_HGRN, d, d), d ** -0.5),
        "ffn_w1": nrm((N_MLA, d, D_FF), d ** -0.5),
        "ffn_w3": nrm((N_MLA, d, D_FF), d ** -0.5),
        "ffn_w2": nrm((N_MLA, D_FF, d), D_FF ** -0.5),
        "moe_router": nrm((N_HGRN, d, N_EXPERTS), d ** -0.5),
        "moe_w1": nrm((N_HGRN, N_EXPERTS, d, D_EXPERT), d ** -0.5),
        "moe_w3": nrm((N_HGRN, N_EXPERTS, d, D_EXPERT), d ** -0.5),
        "moe_w2": nrm((N_HGRN, N_EXPERTS, D_EXPERT, d), D_EXPERT ** -0.5),
        "final_norm": gain((d,)),
    }


def reference(x_prompt, x_sample, cache_ckv, cache_kpe, state_hgrn, page_table, c_prompt, c_sample,
              ada_w, ada_b, norm_g, mla_w_in, mla_q_norm, mla_w_uq, mla_kv_norm, mla_w_uk, mla_w_uv,
              mla_w_o, hg_w_in, hg_lower_bounds, hg_g_norm, hg_w_o, ffn_w1, ffn_w3, ffn_w2,
              moe_router, moe_w1, moe_w3, moe_w2, final_norm):
    seq, dec_seq = x_prompt.shape[1], x_sample.shape[1]
    dec_batch = x_sample.shape[0]
    past_len = page_table.shape[1] * PAGE_SIZE
    cos_p, sin_p = rope_tables(jnp.arange(seq))
    cos_s, sin_s = rope_tables(past_len + jnp.arange(dec_seq))
    lower_bounds = hgrn_lower_bounds(hg_lower_bounds)

    xp, xs = x_prompt, x_sample
    ckv_p, kpe_p, st_p, ckv_s, kpe_s, st_s = [], [], [], [], [], []
    for i in range(DEPTH):
        j = i // 2
        mod_p = ada_modulation(c_prompt, ada_w[i], ada_b[i])
        mod_s = ada_modulation(c_sample, ada_w[i], ada_b[i])
        hp = modulate(rms_norm(xp, norm_g[i, 0]), mod_p[0], mod_p[1])
        hs = modulate(rms_norm(xs, norm_g[i, 0]), mod_s[0], mod_s[1])
        if i % 2 == 0:
            mla_w = (mla_w_in[j], mla_q_norm[j], mla_w_uq[j], mla_kv_norm[j], mla_w_uk[j])
            ql_p, qr_p, ckv_new_p, kpe_new_p = mla_project(hp, *mla_w, cos_p, sin_p)
            lat_p = mla_prompt_attend(ql_p, qr_p, ckv_new_p, kpe_new_p)
            ql_s, qr_s, ckv_new_s, kpe_new_s = mla_project(hs, *mla_w, cos_s, sin_s)
            layer_idx = jnp.full_like(page_table, j)
            past_ckv = cache_ckv[layer_idx, page_table].reshape(dec_batch, past_len, KV_LORA)
            past_kpe = cache_kpe[layer_idx, page_table].reshape(dec_batch, past_len, ROPE_DIM)
            lat_s = mla_sample_attend(ql_s, qr_s, ckv_new_s, kpe_new_s, past_ckv, past_kpe)
            yp = mla_output(lat_p, mla_w_uv[j], mla_w_o[j])
            ys = mla_output(lat_s, mla_w_uv[j], mla_w_o[j])
            ckv_p.append(ckv_new_p)
            kpe_p.append(kpe_new_p)
            ckv_s.append(ckv_new_s)
            kpe_s.append(kpe_new_s)
        else:
            zero_state = jnp.zeros((xp.shape[0], HG_HEADS, HG_KDIM, HG_VDIM), F32)
            yp, sp = hgrn2_mixer(hp, zero_state, hg_w_in[j], lower_bounds[i], hg_g_norm[j], hg_w_o[j])
            ys, ss = hgrn2_mixer(hs, state_hgrn[j], hg_w_in[j], lower_bounds[i], hg_g_norm[j], hg_w_o[j])
            st_p.append(sp)
            st_s.append(ss)
        xp = xp + mod_p[2] * yp
        xs = xs + mod_s[2] * ys

        hp = modulate(rms_norm(xp, norm_g[i, 1]), mod_p[3], mod_p[4])
        hs = modulate(rms_norm(xs, norm_g[i, 1]), mod_s[3], mod_s[4])
        if i % 2 == 0:
            fp = swiglu(hp, ffn_w1[j], ffn_w3[j], ffn_w2[j])
            fs = swiglu(hs, ffn_w1[j], ffn_w3[j], ffn_w2[j])
        else:
            fp = moe_swiglu(hp, moe_router[j], moe_w1[j], moe_w3[j], moe_w2[j])
            fs = moe_swiglu(hs, moe_router[j], moe_w1[j], moe_w3[j], moe_w2[j])
        xp = xp + mod_p[5] * fp
        xs = xs + mod_s[5] * fs

    y_prompt = rms_norm(xp, final_norm)
    y_sample = rms_norm(xs, final_norm)
    new_ckv_prompt = jnp.stack(ckv_p)
    new_kpe_prompt = jnp.stack(kpe_p)
    new_state_prompt = jnp.stack(st_p)
    new_ckv_sample = jnp.stack(ckv_s)
    new_kpe_sample = jnp.stack(kpe_s)
    new_state_sample = jnp.stack(st_s)
    return (y_prompt, y_sample, new_ckv_prompt, new_kpe_prompt, new_state_prompt,
            new_ckv_sample, new_kpe_sample, new_state_sample)
```

```python
import functools

import jax
import jax.numpy as jnp
from jax import lax
from jax.experimental import pallas as pl
from jax.experimental.pallas import tpu as pltpu

F32 = jnp.float32
BF16 = jnp.bfloat16
I32 = jnp.int32

PAGE_SIZE = 128
MLA_HEADS = 32
Q_LORA = 1024
KV_LORA = 512
NOPE_DIM = 128
ROPE_DIM = 64
V_DIM = 128
ROPE_THETA = 10000.0
ATTN_SCALE = (NOPE_DIM + ROPE_DIM) ** -0.5
HG_KDIM = 128
HG_CHUNK = 64
N_EXPERTS = 8
TOP_K = 2
N_ADA = 6
EPS = 1e-6
NEG_INF = -1e30

LANE = 128
MIB = 1024 * 1024
VMEM_CAP_BYTES = 56 * MIB


def _params(n_grid, vmem_mib):
    return pltpu.CompilerParams(
        dimension_semantics=("arbitrary",) * n_grid,
        vmem_limit_bytes=min(vmem_mib * MIB, VMEM_CAP_BYTES))


def _silu(x):
    return x * jax.nn.sigmoid(x)


def _split3(x):
    hi = x.astype(BF16)
    r1 = x - hi.astype(F32)
    mid = r1.astype(BF16)
    lo = (r1 - mid.astype(F32)).astype(BF16)
    return hi, mid, lo


def _mm_kernel(*refs, nk, tk, k_valid, a_silu, n_w, epi):
    it = iter(refs)
    a_ref = next(it)
    w_ref = next(it)
    w3_ref = next(it) if n_w == 2 else None
    bias_ref = next(it) if epi == "bias" else None
    x_ref = next(it) if epi == "resid" else None
    gate_ref = next(it) if epi == "resid" else None
    o_ref = next(it)
    acc_ref = next(it) if nk > 1 else None
    acc3_ref = next(it) if (nk > 1 and n_w == 2) else None

    k = pl.program_id(2)
    a = a_ref[...]
    if a_silu:
        a = _silu(a.astype(F32))
    a = a.astype(BF16)
    ragged = (k_valid % tk) != 0
    if ragged:
        col = k * tk + lax.broadcasted_iota(I32, a.shape, 1)
        a = jnp.where(col < k_valid, a, jnp.zeros_like(a))

    def load_w(ref):
        w = ref[...]
        if ragged:
            row = k * tk + lax.broadcasted_iota(I32, w.shape, 0)
            w = jnp.where(row < k_valid, w, jnp.zeros_like(w))
        return w.astype(BF16)

    d = jnp.dot(a, load_w(w_ref), preferred_element_type=F32)
    d3 = jnp.dot(a, load_w(w3_ref), preferred_element_type=F32) if n_w == 2 else None

    def epilogue(h, h3):
        if epi == "plain":
            o_ref[...] = h.astype(o_ref.dtype)
        elif epi == "bias":
            o_ref[...] = (h + bias_ref[...]).astype(o_ref.dtype)
        elif epi == "swiglu":
            o_ref[...] = (_silu(h) * h3).astype(o_ref.dtype)
        else:
            o_ref[...] = (x_ref[...] + gate_ref[...] * h).astype(o_ref.dtype)

    if nk == 1:
        epilogue(d, d3)
        return

    @pl.when(k == 0)
    def _():
        acc_ref[...] = d
        if n_w == 2:
            acc3_ref[...] = d3

    @pl.when(k > 0)
    def _():
        acc_ref[...] += d
        if n_w == 2:
            acc3_ref[...] += d3

    @pl.when(k == nk - 1)
    def _():
        epilogue(acc_ref[...], acc3_ref[...] if n_w == 2 else None)


def _mm(a, w, *, layer=0, w3=None, bias=None, resid=None, gate=None, gate_col=0, gate_group_rows=None,
        out_dtype=F32, tm, tn, tk, a_silu=False, name="mm"):
    m, kdim = a.shape
    n = w.shape[-1]
    tm = min(tm, m)
    tn = min(tn, n)
    tk = min(tk, kdim)
    nk = pl.cdiv(kdim, tk)
    grid = (pl.cdiv(m, tm), pl.cdiv(n, tn), nk)
    n_w = 2 if w3 is not None else 1
    epi = "swiglu" if w3 is not None else "bias" if bias is not None else "resid" if resid is not None else "plain"

    in_specs = [pl.BlockSpec((tm, tk), lambda i, j, k: (i, k)),
                pl.BlockSpec((None, tk, tn), lambda i, j, k: (layer, k, j))]
    args = [a, w]
    if n_w == 2:
        in_specs.append(pl.BlockSpec((None, tk, tn), lambda i, j, k: (layer, k, j)))
        args.append(w3)
    if epi == "bias":
        in_specs.append(pl.BlockSpec((None, 1, tn), lambda i, j, k: (layer, 0, j)))
        args.append(bias)
    if epi == "resid":
        in_specs.append(pl.BlockSpec((tm, tn), lambda i, j, k: (i, j)))
        args.append(resid)
        r = gate.shape[1]
        nj = n // tn
        if r == 1:
            per = gate_group_rows // tm
            in_specs.append(pl.BlockSpec((None, 1, tn), lambda i, j, k: (i // per, 0, gate_col * nj + j)))
        else:
            in_specs.append(pl.BlockSpec((None, tm, tn), lambda i, j, k: (i, 0, gate_col * nj + j)))
        args.append(gate)

    scratch = []
    if nk > 1:
        scratch = [pltpu.VMEM((tm, tn), F32)] * n_w
    out_b = jnp.dtype(out_dtype).itemsize
    est = (2 * tm * tk * a.dtype.itemsize + 2 * n_w * tk * tn * 4 + n_w * tk * tn * 2
           + (n_w + 1) * tm * tn * 4 + 2 * tm * tn * out_b + (4 * tm * tn * 4 if epi == "resid" else 0))
    kern = functools.partial(_mm_kernel, nk=nk, tk=tk, k_valid=kdim, a_silu=a_silu, n_w=n_w, epi=epi)
    return pl.pallas_call(
        kern,
        out_shape=jax.ShapeDtypeStruct((m, n), out_dtype),
        grid=grid,
        in_specs=in_specs,
        out_specs=pl.BlockSpec((tm, tn), lambda i, j, k: (i, j)),
        scratch_shapes=scratch,
        compiler_params=_params(3, est // MIB + 8),
        name=name,
    )(*args)


def _prenorm_kernel(*refs, with_router):
    if with_router:
        x_ref, g_ref, sh_ref, sc_ref, r_ref, o_ref, route_ref = refs
    else:
        x_ref, g_ref, sh_ref, sc_ref, o_ref = refs
    x = x_ref[...]
    y = x * lax.rsqrt(jnp.mean(x * x, axis=-1, keepdims=True) + EPS) * g_ref[...]
    h = y * (1.0 + sc_ref[...]) + sh_ref[...]
    o_ref[...] = h.astype(o_ref.dtype)
    if with_router:
        h1, h2, _ = _split3(h)
        r = r_ref[...]
        r1, r2, _ = _split3(r)
        logits = (jnp.dot(h1, r1, preferred_element_type=F32)
                  + jnp.dot(h1, r2, preferred_element_type=F32)
                  + jnp.dot(h2, r1, preferred_element_type=F32))
        lane = lax.broadcasted_iota(I32, logits.shape, 1).astype(F32)
        logits = jnp.where(lane < N_EXPERTS, logits, -jnp.inf)
        v1 = jnp.max(logits, axis=-1, keepdims=True)
        i1 = jnp.min(jnp.where(logits == v1, lane, float(LANE)), axis=-1, keepdims=True)
        rest = jnp.where(lane == i1, -jnp.inf, logits)
        v2 = jnp.max(rest, axis=-1, keepdims=True)
        i2 = jnp.min(jnp.where(rest == v2, lane, float(LANE)), axis=-1, keepdims=True)
        e2 = jnp.exp(v2 - v1)
        w1 = 1.0 / (1.0 + e2)
        w2 = e2 / (1.0 + e2)
        route = jnp.where(lane == 0, i1, jnp.where(lane == 1, i2, jnp.where(lane == 2, w1, jnp.where(lane == 3, w2, 0.0))))
        route_ref[...] = route


def _prenorm(x3, norm_g4, g_idx, mod, shift_col, router=None, *, tt):
    b, t, d = x3.shape
    tt = min(tt, t)
    r = mod.shape[1]
    if r == 1:
        mod_spec = lambda c: pl.BlockSpec((None, 1, d), lambda bi, ti: (bi, 0, c))
    else:
        mod_spec = lambda c: pl.BlockSpec((None, tt, d), lambda bi, ti: (bi, ti, c))
    in_specs = [pl.BlockSpec((None, tt, d), lambda bi, ti: (bi, ti, 0)),
                pl.BlockSpec((None, 1, d), lambda bi, ti: (g_idx, 0, 0)),
                mod_spec(shift_col), mod_spec(shift_col + 1)]
    args = [x3, norm_g4, mod, mod]
    out_shape = [jax.ShapeDtypeStruct((b, t, d), BF16)]
    out_specs = [pl.BlockSpec((None, tt, d), lambda bi, ti: (bi, ti, 0))]
    if router is not None:
        in_specs.append(pl.BlockSpec((d, LANE), lambda bi, ti: (0, 0)))
        args.append(router)
        out_shape.append(jax.ShapeDtypeStruct((b, t, LANE), F32))
        out_specs.append(pl.BlockSpec((None, tt, LANE), lambda bi, ti: (bi, ti, 0)))
    res = pl.pallas_call(
        functools.partial(_prenorm_kernel, with_router=router is not None),
        out_shape=out_shape, grid=(b, t // tt), in_specs=in_specs, out_specs=out_specs,
        compiler_params=_params(2, 48), name="prenorm_router" if router is not None else "prenorm",
    )(*args)
    return res if router is not None else res[0]


def _mla_post_kernel(p_ref, qn_ref, kn_ref, cos_ref, sin_ref, cq_ref, ckv_ref, kpe_ref):
    cq = p_ref[:, :Q_LORA]
    cq = cq * lax.rsqrt(jnp.mean(cq * cq, axis=-1, keepdims=True) + EPS) * qn_ref[...]
    cq_ref[...] = cq.astype(cq_ref.dtype)
    c = p_ref[:, Q_LORA:Q_LORA + KV_LORA]
    ckv_ref[...] = c * lax.rsqrt(jnp.mean(c * c, axis=-1, keepdims=True) + EPS) * kn_ref[...]
    slab = p_ref[:, Q_LORA + KV_LORA:]
    rot = pltpu.roll(slab, ROPE_DIM, 1)
    kpe = slab * cos_ref[...] + rot * sin_ref[...]
    kpe_ref[...] = kpe[:, :ROPE_DIM]


def _mla_post(proj3, q_norm, kv_norm, cos_p, sin_p, *, tt):
    b, t, n = proj3.shape
    tt = min(tt, t)
    return pl.pallas_call(
        _mla_post_kernel,
        out_shape=[jax.ShapeDtypeStruct((b, t, Q_LORA), BF16),
                   jax.ShapeDtypeStruct((b, t, KV_LORA), F32),
                   jax.ShapeDtypeStruct((b, t, ROPE_DIM), F32)],
        grid=(b, t // tt),
        in_specs=[pl.BlockSpec((None, tt, n), lambda bi, ti: (bi, ti, 0)),
                  pl.BlockSpec((1, Q_LORA), lambda bi, ti: (0, 0)),
                  pl.BlockSpec((1, KV_LORA), lambda bi, ti: (0, 0)),
                  pl.BlockSpec((tt, LANE), lambda bi, ti: (ti, 0)),
                  pl.BlockSpec((tt, LANE), lambda bi, ti: (ti, 0))],
        out_specs=[pl.BlockSpec((None, tt, Q_LORA), lambda bi, ti: (bi, ti, 0)),
                   pl.BlockSpec((None, tt, KV_LORA), lambda bi, ti: (bi, ti, 0)),
                   pl.BlockSpec((None, tt, ROPE_DIM), lambda bi, ti: (bi, ti, 0))],
        compiler_params=_params(2, 32), name="mla_post",
    )(proj3, q_norm, kv_norm, cos_p, sin_p)


def _attn_prompt_kernel(q_ref, cos_ref, sin_ref, kt_ref, v_ref, wuk_ref, wuv_ref, o_ref,
                        qs_ref, m_ref, l_ref, acc_ref, *, tq, tkv):
    qi = pl.program_id(1)
    hn = MLA_HEADS * NOPE_DIM
    hr = MLA_HEADS * ROPE_DIM
    n_pair = MLA_HEADS // 2
    cos_t = jnp.concatenate([cos_ref[...]] * (hr // LANE), axis=1)
    sin_t = jnp.concatenate([sin_ref[...]] * (hr // LANE), axis=1)
    qpe = (q_ref[:, hn:hn + hr].astype(F32) * cos_t
           + q_ref[:, hn + hr:hn + 2 * hr].astype(F32) * sin_t).astype(BF16)
    n_full = (qi * tq) // tkv
    rows = n_pair * tq

    for par in range(2):
        for mi in range(n_pair):
            h = 2 * mi + par
            ql = jnp.dot(q_ref[:, h * NOPE_DIM:(h + 1) * NOPE_DIM], wuk_ref[h], preferred_element_type=F32)
            qs_ref[mi * tq:(mi + 1) * tq, :KV_LORA] = ql.astype(BF16)
            qs_ref[mi * tq:(mi + 1) * tq, KV_LORA:] = qpe[:, mi * LANE:(mi + 1) * LANE]
        m_ref[...] = jnp.full(m_ref.shape, -jnp.inf, F32)
        l_ref[...] = jnp.zeros(l_ref.shape, F32)
        acc_ref[...] = jnp.zeros(acc_ref.shape, F32)

        def step(j, masked):
            s = jnp.dot(qs_ref[...], kt_ref[par, j], preferred_element_type=F32) * ATTN_SCALE
            if masked:
                r = lax.broadcasted_iota(I32, (rows, tkv), 0)
                q_pos = qi * tq + (r & (tq - 1))
                k_pos = j * tkv + lax.broadcasted_iota(I32, (rows, tkv), 1)
                s = jnp.where(k_pos <= q_pos, s, NEG_INF)
            m_prev = m_ref[...]
            m_new = jnp.maximum(m_prev, jnp.max(s, axis=1, keepdims=True))
            alpha = jnp.exp(m_prev - m_new)
            p = jnp.exp(s - jnp.tile(m_new, (1, tkv // LANE)))
            l_ref[...] = alpha * l_ref[...] + jnp.sum(p, axis=1, keepdims=True)
            pv = jnp.dot(p.astype(BF16), v_ref[j], preferred_element_type=F32)
            acc_ref[...] = acc_ref[...] * jnp.tile(alpha, (1, KV_LORA // LANE)) + pv
            m_ref[...] = m_new

        def body(j, carry):
            step(j, False)
            return carry

        lax.fori_loop(0, n_full, body, 0)
        step(n_full, True)

        inv = 1.0 / l_ref[...]
        for mi in range(n_pair):
            h = 2 * mi + par
            sl = slice(mi * tq, (mi + 1) * tq)
            lat = (acc_ref[sl, :] * jnp.tile(inv[sl, :], (1, KV_LORA // LANE))).astype(BF16)
            o_ref[:, h * V_DIM:(h + 1) * V_DIM] = jnp.dot(
                lat, wuv_ref[h], preferred_element_type=F32).astype(o_ref.dtype)


def _attn_prompt(q3, cos128, sin128, kt, v4, wuk, wuv, *, tq, tkv):
    b, t, nq = q3.shape
    nkv = t // tkv
    kc = KV_LORA + LANE
    rows = (MLA_HEADS // 2) * tq
    one = pl.Buffered(1)
    return pl.pallas_call(
        functools.partial(_attn_prompt_kernel, tq=tq, tkv=tkv),
        out_shape=jax.ShapeDtypeStruct((b, t, MLA_HEADS * V_DIM), BF16),
        grid=(b, t // tq),
        in_specs=[pl.BlockSpec((None, tq, nq), lambda bi, qi: (bi, qi, 0)),
                  pl.BlockSpec((tq, LANE), lambda bi, qi: (qi, 0)),
                  pl.BlockSpec((tq, LANE), lambda bi, qi: (qi, 0)),
                  pl.BlockSpec((None, 2, nkv, kc, tkv), lambda bi, qi: (bi, 0, 0, 0, 0), pipeline_mode=one),
                  pl.BlockSpec((None, nkv, tkv, KV_LORA), lambda bi, qi: (bi, 0, 0, 0), pipeline_mode=one),
                  pl.BlockSpec((MLA_HEADS, NOPE_DIM, KV_LORA), lambda bi, qi: (0, 0, 0), pipeline_mode=one),
                  pl.BlockSpec((MLA_HEADS, KV_LORA, V_DIM), lambda bi, qi: (0, 0, 0), pipeline_mode=one)],
        out_specs=pl.BlockSpec((None, tq, MLA_HEADS * V_DIM), lambda bi, qi: (bi, qi, 0)),
        scratch_shapes=[pltpu.VMEM((rows, kc), BF16), pltpu.VMEM((rows, LANE), F32),
                        pltpu.VMEM((rows, LANE), F32), pltpu.VMEM((rows, KV_LORA), F32)],
        compiler_params=_params(2, 56), name="mla_attn_prompt",
    )(q3, cos128, sin128, kt, v4, wuk, wuv)


def _head_mm_kernel(a_ref, w_ref, o_ref):
    o_ref[...] = jnp.dot(a_ref[...].astype(BF16), w_ref[...], preferred_element_type=F32).astype(o_ref.dtype)


def _head_mm(a, w, ka, na):
    m = a.shape[0]
    h = w.shape[0]
    return pl.pallas_call(
        _head_mm_kernel,
        out_shape=jax.ShapeDtypeStruct((m, h * na), BF16),
        grid=(h,),
        in_specs=[pl.BlockSpec((m, ka), lambda hi: (0, hi)),
                  pl.BlockSpec((None, ka, na), lambda hi: (hi, 0, 0))],
        out_specs=pl.BlockSpec((m, na), lambda hi: (0, hi)),
        compiler_params=_params(1, 16), name="head_mm",
    )(a, w)


def _rope_q_kernel(x_ref, xr_ref, cos_ref, sin_ref, o_ref):
    o_ref[...] = (x_ref[...].astype(F32) * cos_ref[...] + xr_ref[...].astype(F32) * sin_ref[...]).astype(o_ref.dtype)


def _rope_q(q, cos_t, sin_t):
    m = q.shape[0]
    hr = MLA_HEADS * ROPE_DIM
    nope_blocks = (MLA_HEADS * NOPE_DIM) // hr
    return pl.pallas_call(
        _rope_q_kernel,
        out_shape=jax.ShapeDtypeStruct((m, hr), BF16),
        grid=(1,),
        in_specs=[pl.BlockSpec((m, hr), lambda i: (0, nope_blocks)),
                  pl.BlockSpec((m, hr), lambda i: (0, nope_blocks + 1)),
                  pl.BlockSpec((m, hr), lambda i: (0, 0)),
                  pl.BlockSpec((m, hr), lambda i: (0, 0))],
        out_specs=pl.BlockSpec((m, hr), lambda i: (0, 0)),
        compiler_params=_params(1, 16), name="rope_q",
    )(q, q, cos_t, sin_t)


def _attn_decode_kernel(pt_ref, ql_ref, qp_ref, cn_ref, kn_ref, *rest, pps, n_steps):
    ckv_refs = rest[:pps]
    kpe_refs = rest[pps:2 * pps]
    o_ref = rest[2 * pps]
    m_ref, l_ref, acc_ref = rest[2 * pps + 1:]
    c = pl.program_id(1)
    ql = ql_ref[...]
    qp = qp_ref[...]

    @pl.when(c == 0)
    def _():
        cn = cn_ref[...]
        kn = kn_ref[...]
        s_new = (jnp.sum(ql.astype(F32) * cn, axis=-1, keepdims=True)
                 + jnp.sum(qp.astype(F32) * kn, axis=-1, keepdims=True)) * ATTN_SCALE
        m_ref[...] = jnp.broadcast_to(s_new, m_ref.shape)
        l_ref[...] = jnp.ones(l_ref.shape, F32)
        acc_ref[...] = jnp.broadcast_to(cn, acc_ref.shape)

    nt = (((1,), (1,)), ((), ()))
    ks = []
    ss = []
    for p in range(pps):
        kb = ckv_refs[p][...].astype(BF16)
        pb = kpe_refs[p][...].astype(BF16)
        ks.append(kb)
        ss.append(lax.dot_general(ql, kb, nt, preferred_element_type=F32)
                  + lax.dot_general(qp, pb, nt, preferred_element_type=F32))
    s = jnp.concatenate(ss, axis=1) * ATTN_SCALE
    m_prev = m_ref[...]
    m_new = jnp.maximum(m_prev, jnp.max(s, axis=1, keepdims=True))
    alpha = jnp.exp(m_prev - m_new)
    p_all = jnp.exp(s - jnp.tile(m_new, (1, pps))).astype(BF16)
    l_ref[...] = alpha * l_ref[...] + jnp.sum(p_all.astype(F32), axis=1, keepdims=True)
    pv = jnp.dot(p_all[:, :PAGE_SIZE], ks[0], preferred_element_type=F32)
    for p in range(1, pps):
        pv += jnp.dot(p_all[:, p * PAGE_SIZE:(p + 1) * PAGE_SIZE], ks[p], preferred_element_type=F32)
    acc_ref[...] = acc_ref[...] * jnp.tile(alpha, (1, KV_LORA // LANE)) + pv
    m_ref[...] = m_new

    @pl.when(c == n_steps - 1)
    def _():
        o_ref[...] = (acc_ref[...] / jnp.tile(l_ref[...], (1, KV_LORA // LANE))).astype(o_ref.dtype)


def _attn_decode(page_table, ql3, qp3, ckv_new3, kpe_new3, cache_ckv, cache_kpe, layer, *, pps):
    nb, n_pages = page_table.shape
    n_steps = n_pages // pps
    h = MLA_HEADS

    def page_spec(width, p):
        return pl.BlockSpec((None, None, PAGE_SIZE, width),
                            lambda bi, ci, pt: (layer, pt[bi, ci * pps + p], 0, 0))

    in_specs = [pl.BlockSpec((None, h, KV_LORA), lambda bi, ci, pt: (bi, 0, 0)),
                pl.BlockSpec((None, h, ROPE_DIM), lambda bi, ci, pt: (bi, 0, 0)),
                pl.BlockSpec((None, 1, KV_LORA), lambda bi, ci, pt: (bi, 0, 0)),
                pl.BlockSpec((None, 1, ROPE_DIM), lambda bi, ci, pt: (bi, 0, 0))]
    in_specs += [page_spec(KV_LORA, p) for p in range(pps)]
    in_specs += [page_spec(ROPE_DIM, p) for p in range(pps)]
    grid_spec = pltpu.PrefetchScalarGridSpec(
        num_scalar_prefetch=1, grid=(nb, n_steps), in_specs=in_specs,
        out_specs=pl.BlockSpec((None, h, KV_LORA), lambda bi, ci, pt: (bi, 0, 0)),
        scratch_shapes=[pltpu.VMEM((h, LANE), F32), pltpu.VMEM((h, LANE), F32), pltpu.VMEM((h, KV_LORA), F32)])
    return pl.pallas_call(
        functools.partial(_attn_decode_kernel, pps=pps, n_steps=n_steps),
        out_shape=jax.ShapeDtypeStruct((nb, h, KV_LORA), BF16),
        grid_spec=grid_spec,
        compiler_params=_params(2, 32), name="mla_attn_decode",
    )(page_table, ql3, qp3, ckv_new3, kpe_new3, *([cache_ckv] * pps), *([cache_kpe] * pps))


def _hgrn_prompt_kernel(q_ref, f_ref, i_ref, lb_ref, o_ref, s_out_ref, s_ref, *, tt, hb, n_tb):
    tb = pl.program_id(2)

    @pl.when(tb == 0)
    def _():
        s_ref[...] = jnp.zeros(s_ref.shape, F32)

    c = HG_CHUNK
    dk = HG_KDIM
    lb = lb_ref[...]
    sg = jax.nn.sigmoid(f_ref[...])
    log_f = jnp.log(lb + (1.0 - lb) * sg)
    kk = (1.0 - lb) * (1.0 - sg)
    qs = _silu(q_ref[...])
    vv = i_ref[...]
    row = lax.broadcasted_iota(I32, (c, c), 0)
    col = lax.broadcasted_iota(I32, (c, c), 1)
    causal = col <= row
    tril = causal.astype(BF16)
    mid = (c - 1) // 2
    nt = (((1,), (1,)), ((), ()))

    for ci in range(tt // c):
        rs = slice(ci * c, (ci + 1) * c)
        lf1, lf2, lf3 = _split3(log_f[rs])
        cum = (jnp.dot(tril, lf1, preferred_element_type=F32)
               + jnp.dot(tril, lf2, preferred_element_type=F32)
               + jnp.dot(tril, lf3, preferred_element_type=F32))
        ref = cum[mid:mid + 1]
        last = cum[c - 1:c]
        qc = qs[rs]
        kc = kk[rs]
        vb = vv[rs].astype(BF16)
        qd = (qc * jnp.exp(cum - ref)).astype(BF16)
        kd = (kc * jnp.exp(ref - cum)).astype(BF16)
        q0 = (qc * jnp.exp(cum)).astype(BF16)
        k2 = kc * jnp.exp(last - cum)
        dec = jnp.exp(last)
        for h in range(hb):
            ls = slice(h * dk, (h + 1) * dk)
            a = lax.dot_general(qd[:, ls], kd[:, ls], nt, preferred_element_type=F32)
            a = jnp.where(causal, a, 0.0).astype(BF16)
            s_prev = s_ref[h]
            o = (jnp.dot(q0[:, ls], s_prev.astype(BF16), preferred_element_type=F32)
                 + jnp.dot(a, vb[:, ls], preferred_element_type=F32))
            o_ref[rs, ls] = o
            dec_col = jnp.broadcast_to(dec[:, ls], (dk, dk)).T
            k2t = k2[:, ls].T.astype(BF16)
            s_ref[h] = dec_col * s_prev + jnp.dot(k2t, vb[:, ls], preferred_element_type=F32)

    @pl.when(tb == n_tb - 1)
    def _():
        s_out_ref[...] = s_ref[...]


def _hgrn_prompt(proj3, lb2, *, tt, hb):
    b, t, n4 = proj3.shape
    d = n4 // 4
    heads = d // HG_KDIM
    w = hb * HG_KDIM
    nhg = heads // hb
    n_tb = t // tt
    return pl.pallas_call(
        functools.partial(_hgrn_prompt_kernel, tt=tt, hb=hb, n_tb=n_tb),
        out_shape=[jax.ShapeDtypeStruct((b, t, d), F32),
                   jax.ShapeDtypeStruct((b, heads, HG_KDIM, HG_KDIM), F32)],
        grid=(b, nhg, n_tb),
        in_specs=[pl.BlockSpec((None, tt, w), lambda bi, hi, ti: (bi, ti, hi)),
                  pl.BlockSpec((None, tt, w), lambda bi, hi, ti: (bi, ti, nhg + hi)),
                  pl.BlockSpec((None, tt, w), lambda bi, hi, ti: (bi, ti, 2 * nhg + hi)),
                  pl.BlockSpec((1, w), lambda bi, hi, ti: (0, hi))],
        out_specs=[pl.BlockSpec((None, tt, w), lambda bi, hi, ti: (bi, ti, hi)),
                   pl.BlockSpec((None, hb, HG_KDIM, HG_KDIM), lambda bi, hi, ti: (bi, hi, 0, 0))],
        scratch_shapes=[pltpu.VMEM((hb, HG_KDIM, HG_KDIM), F32)],
        compiler_params=_params(3, 32), name="hgrn_prompt",
    )(proj3, proj3, proj3, lb2)


def _hgrn_decode_kernel(p_ref, lb_ref, s_ref, o_ref, s_out_ref):
    heads = s_ref.shape[0]
    dk = HG_KDIM
    lb = lb_ref[...]
    sg = jax.nn.sigmoid(p_ref[1])
    fg = lb + (1.0 - lb) * sg
    kk = (1.0 - lb) * (1.0 - sg)
    qs = _silu(p_ref[0])
    vv = p_ref[2]
    outs = []
    for h in range(heads):
        f_col = jnp.broadcast_to(fg[h:h + 1], (dk, dk)).T
        k_col = jnp.broadcast_to(kk[h:h + 1], (dk, dk)).T
        q_col = jnp.broadcast_to(qs[h:h + 1], (dk, dk)).T
        s_new = f_col * s_ref[h] + k_col * vv[h:h + 1]
        s_out_ref[h] = s_new
        outs.append(jnp.sum(q_col * s_new, axis=0, keepdims=True))
    o_ref[...] = jnp.concatenate(outs, axis=0)


def _hgrn_decode(proj4, lb2, state):
    b, _, heads, dk = proj4.shape
    return pl.pallas_call(
        _hgrn_decode_kernel,
        out_shape=[jax.ShapeDtypeStruct((b, heads, dk), F32),
                   jax.ShapeDtypeStruct(state.shape, F32)],
        grid=(b,),
        in_specs=[pl.BlockSpec((None, 4, heads, dk), lambda bi: (bi, 0, 0, 0)),
                  pl.BlockSpec((heads, dk), lambda bi: (0, 0)),
                  pl.BlockSpec((None, heads, dk, dk), lambda bi: (bi, 0, 0, 0))],
        out_specs=[pl.BlockSpec((None, heads, dk), lambda bi: (bi, 0, 0)),
                   pl.BlockSpec((None, heads, dk, dk), lambda bi: (bi, 0, 0, 0))],
        compiler_params=_params(1, 32), name="hgrn_decode",
    )(proj4, lb2, state)


def _gated_norm_kernel(o_ref, g_ref, gn_ref, out_ref):
    o = o_ref[...]
    y = o * lax.rsqrt(jnp.mean(o * o, axis=-1, keepdims=True) + EPS) * gn_ref[...]
    out_ref[...] = (y * _silu(g_ref[...])).astype(out_ref.dtype)


def _gated_norm(o3, proj3, g_norm, *, tt):
    b, t, d = o3.shape
    tt = min(tt, t)
    return pl.pallas_call(
        _gated_norm_kernel,
        out_shape=jax.ShapeDtypeStruct((b, t, d), BF16),
        grid=(b, t // tt),
        in_specs=[pl.BlockSpec((None, tt, d), lambda bi, ti: (bi, ti, 0)),
                  pl.BlockSpec((None, tt, d), lambda bi, ti: (bi, ti, 3)),
                  pl.BlockSpec((1, d), lambda bi, ti: (0, 0))],
        out_specs=pl.BlockSpec((None, tt, d), lambda bi, ti: (bi, ti, 0)),
        compiler_params=_params(2, 48), name="gated_norm",
    )(o3, proj3, g_norm)


def _moe_up_kernel(te_ref, nv_ref, a_ref, w1_ref, w3_ref, o_ref):
    t = pl.program_id(1)

    @pl.when(t < nv_ref[0])
    def _():
        a = a_ref[...]
        h1 = jnp.dot(a, w1_ref[...].astype(BF16), preferred_element_type=F32)
        h3 = jnp.dot(a, w3_ref[...].astype(BF16), preferred_element_type=F32)
        o_ref[...] = (_silu(h1) * h3).astype(o_ref.dtype)

    @pl.when(t >= nv_ref[0])
    def _():
        o_ref[...] = jnp.zeros(o_ref.shape, o_ref.dtype)


def _moe_up(tile_expert, n_valid, xs, w1, w3, *, tm, tn):
    r, d = xs.shape
    n = w1.shape[-1]
    n_tiles = r // tm
    grid_spec = pltpu.PrefetchScalarGridSpec(
        num_scalar_prefetch=2, grid=(n // tn, n_tiles),
        in_specs=[pl.BlockSpec((tm, d), lambda j, t, te, nv: (jnp.minimum(t, nv[0] - 1), 0)),
                  pl.BlockSpec((None, d, tn), lambda j, t, te, nv: (te[t], 0, j)),
                  pl.BlockSpec((None, d, tn), lambda j, t, te, nv: (te[t], 0, j))],
        out_specs=pl.BlockSpec((tm, tn), lambda j, t, te, nv: (t, j)))
    est = 2 * tm * d * 2 + 4 * d * tn * 4 + 2 * d * tn * 2 + 2 * tm * tn * 2 + 3 * tm * tn * 4
    return pl.pallas_call(
        _moe_up_kernel,
        out_shape=jax.ShapeDtypeStruct((r, n), BF16),
        grid_spec=grid_spec,
        compiler_params=_params(2, est // MIB + 6), name="moe_up",
    )(tile_expert, n_valid, xs, w1, w3)


def _moe_down_kernel(te_ref, nv_ref, a_ref, w_ref, o_ref):
    t = pl.program_id(0)
    k = pl.program_id(1)
    valid = t < nv_ref[0]

    @pl.when(valid & (k == 0))
    def _():
        o_ref[...] = jnp.dot(a_ref[...], w_ref[...].astype(BF16), preferred_element_type=F32)

    @pl.when(valid & (k > 0))
    def _():
        o_ref[...] += jnp.dot(a_ref[...], w_ref[...].astype(BF16), preferred_element_type=F32)

    @pl.when(jnp.logical_not(valid) & (k == 0))
    def _():
        o_ref[...] = jnp.zeros(o_ref.shape, o_ref.dtype)


def _moe_down(tile_expert, n_valid, act, w2, *, tm, tk):
    r, kdim = act.shape
    n = w2.shape[-1]
    nk = kdim // tk

    def kk(t, k, nv):
        return jnp.where(t < nv[0], k, nk - 1)

    grid_spec = pltpu.PrefetchScalarGridSpec(
        num_scalar_prefetch=2, grid=(r // tm, nk),
        in_specs=[pl.BlockSpec((tm, tk), lambda t, k, te, nv: (jnp.minimum(t, nv[0] - 1), kk(t, k, nv))),
                  pl.BlockSpec((None, tk, n), lambda t, k, te, nv: (te[t], kk(t, k, nv), 0))],
        out_specs=pl.BlockSpec((tm, n), lambda t, k, te, nv: (t, 0)))
    est = 2 * tm * tk * 2 + 2 * tk * n * 4 + tk * n * 2 + 2 * tm * n * 4 + tm * n * 4
    return pl.pallas_call(
        _moe_down_kernel,
        out_shape=jax.ShapeDtypeStruct((r, n), F32),
        grid_spec=grid_spec,
        compiler_params=_params(2, est // MIB + 6), name="moe_down",
    )(tile_expert, n_valid, act, w2)


def _moe_combine_kernel(x_ref, gate_ref, y_ref, w_ref, fn_ref, o_ref):
    w = w_ref[...]
    y = w[:, 0:1] * y_ref[0] + w[:, 1:2] * y_ref[1]
    x = x_ref[...] + gate_ref[...] * y
    o_ref[...] = x * lax.rsqrt(jnp.mean(x * x, axis=-1, keepdims=True) + EPS) * fn_ref[...]


def _moe_combine_final(x3, mod, gate_col, y4, w3, final_norm, *, tt):
    b, t, d = x3.shape
    tt = min(tt, t)
    r = mod.shape[1]
    if r == 1:
        gate_spec = pl.BlockSpec((None, 1, d), lambda bi, ti: (bi, 0, gate_col))
    else:
        gate_spec = pl.BlockSpec((None, tt, d), lambda bi, ti: (bi, ti, gate_col))
    return pl.pallas_call(
        _moe_combine_kernel,
        out_shape=jax.ShapeDtypeStruct((b, t, d), F32),
        grid=(b, t // tt),
        in_specs=[pl.BlockSpec((None, tt, d), lambda bi, ti: (bi, ti, 0)),
                  gate_spec,
                  pl.BlockSpec((TOP_K, None, tt, d), lambda bi, ti: (0, bi, ti, 0)),
                  pl.BlockSpec((None, tt, LANE), lambda bi, ti: (bi, ti, 0)),
                  pl.BlockSpec((1, d), lambda bi, ti: (0, 0))],
        out_specs=pl.BlockSpec((None, tt, d), lambda bi, ti: (bi, ti, 0)),
        compiler_params=_params(2, 48), name="moe_combine_final",
    )(x3, mod, y4, w3, final_norm)


MOE_TM = 512


def kernel(x_prompt, x_sample, cache_ckv, cache_kpe, state_hgrn, page_table, c_prompt, c_sample,
           ada_w, ada_b, norm_g, mla_w_in, mla_q_norm, mla_w_uq, mla_kv_norm, mla_w_uk, mla_w_uv,
           mla_w_o, hg_w_in, hg_lower_bounds, hg_g_norm, hg_w_o, ffn_w1, ffn_w3, ffn_w2,
           moe_router, moe_w1, moe_w3, moe_w2, final_norm):
    bp, seq, d = x_prompt.shape
    bs, dec_seq, _ = x_sample.shape
    depth = ada_w.shape[0]
    assert depth == 2 and dec_seq == 1
    mp = bp * seq
    past_len = page_table.shape[1] * PAGE_SIZE
    heads = MLA_HEADS
    hg_heads = d // HG_KDIM

    n_c = bp + bs
    pad_c = (-n_c) % 8
    c_all = jnp.concatenate([c_prompt, c_sample, jnp.zeros((pad_c, d), F32)], axis=0)
    ada_b3 = ada_b.reshape(depth, 1, N_ADA * d)
    mods_p, mods_s = [], []
    for i in range(depth):
        mod = _mm(c_all, ada_w, layer=i, bias=ada_b3, out_dtype=F32, tm=n_c + pad_c, tn=1024, tk=2048,
                  a_silu=True, name="ada_mod")
        mods_p.append(mod[:bp].reshape(bp, 1, N_ADA * d))
        mods_s.append(mod[bp:bp + bs].reshape(1, bs, N_ADA * d))

    norm_g4 = norm_g.reshape(depth * 2, 1, d)
    xp = x_prompt
    xs = x_sample.reshape(1, bs, d)

    inv = ROPE_THETA ** (-jnp.arange(0, ROPE_DIM, 2, dtype=F32) / ROPE_DIM)

    def tables(pos):
        ang = pos.astype(F32)[:, None] * inv[None, :]
        cos2 = jnp.concatenate([jnp.cos(ang)] * 2, axis=1)
        sin2 = jnp.concatenate([jnp.sin(ang)] * 2, axis=1)
        return cos2, sin2

    cos_p, sin_p = tables(jnp.arange(seq))
    cos_s, sin_s = tables(jnp.broadcast_to(past_len + jnp.arange(dec_seq), (bs,)))
    zpad = lambda a: jnp.concatenate([a, jnp.zeros_like(a)], axis=1)
    dup = lambda a: jnp.concatenate([a, a], axis=1)

    w_in = mla_w_in[0]
    kpe_w = w_in[:, Q_LORA + KV_LORA:]
    half = ROPE_DIM // 2
    w_in_aug = jnp.concatenate([w_in, -kpe_w[:, half:], kpe_w[:, :half]], axis=1)[None]
    wq = mla_w_uq[0].reshape(Q_LORA, heads, NOPE_DIM + ROPE_DIM)
    wq_rope = wq[:, :, NOPE_DIM:]
    wq_rot = jnp.concatenate([-wq_rope[..., half:], wq_rope[..., :half]], axis=-1)
    w_uq_aug = jnp.concatenate([wq[:, :, :NOPE_DIM].reshape(Q_LORA, -1), wq_rope.reshape(Q_LORA, -1),
                                wq_rot.reshape(Q_LORA, -1)], axis=1)[None]
    wuk = jnp.transpose(mla_w_uk[0], (1, 2, 0)).astype(BF16)
    wuv = jnp.transpose(mla_w_uv[0], (1, 0, 2)).astype(BF16)
    q_norm = mla_q_norm[0].reshape(1, Q_LORA)
    kv_norm = mla_kv_norm[0].reshape(1, KV_LORA)

    hp = _prenorm(xp, norm_g4, 0, mods_p[0], 0, tt=256)
    proj_p = _mm(hp.reshape(mp, d), w_in_aug, out_dtype=F32, tm=1024, tn=2048, tk=1024, name="mla_in")
    cq_p, ckv_p, kpe_p = _mla_post(proj_p.reshape(bp, seq, -1), q_norm, kv_norm, zpad(cos_p), zpad(sin_p), tt=512)
    q_p = _mm(cq_p.reshape(mp, Q_LORA), w_uq_aug, out_dtype=BF16, tm=2048, tn=1024, tk=1024, name="mla_uq")
    tkv = 512
    nkv = seq // tkv
    ckv_b = ckv_p.astype(BF16)
    kpe_b = kpe_p.astype(BF16)
    zk = jnp.zeros_like(kpe_b)
    kcat = jnp.stack([jnp.concatenate([ckv_b, kpe_b, zk], axis=-1),
                      jnp.concatenate([ckv_b, zk, kpe_b], axis=-1)], axis=1)
    kt = jnp.swapaxes(kcat.reshape(bp, 2, nkv, tkv, KV_LORA + LANE), -1, -2)
    v4 = ckv_b.reshape(bp, nkv, tkv, KV_LORA)
    o_p = _attn_prompt(q_p.reshape(bp, seq, -1), dup(cos_p), dup(sin_p), kt, v4, wuk, wuv, tq=128, tkv=tkv)
    xp = _mm(o_p.reshape(mp, d), mla_w_o, resid=xp.reshape(mp, d), gate=mods_p[0], gate_col=2,
             gate_group_rows=seq, out_dtype=F32, tm=1024, tn=1024, tk=1024, name="mla_o").reshape(bp, seq, d)

    hs = _prenorm(xs, norm_g4, 0, mods_s[0], 0, tt=bs)
    proj_s = _mm(hs.reshape(bs, d), w_in_aug, out_dtype=F32, tm=bs, tn=2048, tk=2048, name="mla_in_s")
    cq_s, ckv_s, kpe_s = _mla_post(proj_s.reshape(1, bs, -1), q_norm, kv_norm, zpad(cos_s), zpad(sin_s), tt=bs)
    q_s = _mm(cq_s.reshape(bs, Q_LORA), w_uq_aug, out_dtype=BF16, tm=bs, tn=2048, tk=1024, name="mla_uq_s")
    ql_s = _head_mm(q_s, wuk, NOPE_DIM, KV_LORA).reshape(bs, heads, KV_LORA)
    qp_s = _rope_q(q_s, jnp.tile(cos_s, (1, heads)), jnp.tile(sin_s, (1, heads))).reshape(bs, heads, ROPE_DIM)
    lat_s = _attn_decode(page_table, ql_s, qp_s, ckv_s.reshape(bs, 1, KV_LORA), kpe_s.reshape(bs, 1, ROPE_DIM),
                         cache_ckv, cache_kpe, 0, pps=16)
    o_s = _head_mm(lat_s.reshape(bs, heads * KV_LORA), wuv, KV_LORA, V_DIM)
    xs = _mm(o_s, mla_w_o, resid=xs.reshape(bs, d), gate=mods_s[0], gate_col=2, out_dtype=F32,
             tm=bs, tn=1024, tk=2048, name="mla_o_s").reshape(1, bs, d)

    hp = _prenorm(xp, norm_g4, 1, mods_p[0], 3, tt=256)
    act_p = _mm(hp.reshape(mp, d), ffn_w1, w3=ffn_w3, out_dtype=BF16, tm=2048, tn=1024, tk=512, name="ffn_up")
    xp = _mm(act_p, ffn_w2, resid=xp.reshape(mp, d), gate=mods_p[0], gate_col=5, gate_group_rows=seq,
             out_dtype=F32, tm=1024, tn=1024, tk=1024, name="ffn_down").reshape(bp, seq, d)
    hs = _prenorm(xs, norm_g4, 1, mods_s[0], 3, tt=bs)
    act_s = _mm(hs.reshape(bs, d), ffn_w1, w3=ffn_w3, out_dtype=BF16, tm=bs, tn=1024, tk=1024, name="ffn_up_s")
    xs = _mm(act_s, ffn_w2, resid=xs.reshape(bs, d), gate=mods_s[0], gate_col=5, out_dtype=F32,
             tm=bs, tn=1024, tk=2048, name="ffn_down_s").reshape(1, bs, d)

    p_lb = jax.nn.softmax(hg_lower_bounds.astype(F32), axis=0)
    lb = (jnp.cumsum(p_lb, axis=0) - p_lb[0:1])[1].reshape(1, d)

    hp = _prenorm(xp, norm_g4, 2, mods_p[1], 0, tt=256)
    proj_p = _mm(hp.reshape(mp, d), hg_w_in, out_dtype=F32, tm=2048, tn=1024, tk=512, name="hg_in").reshape(bp, seq, 4 * d)
    o_p, st_p = _hgrn_prompt(proj_p, lb, tt=256, hb=4)
    og_p = _gated_norm(o_p, proj_p, hg_g_norm, tt=256)
    xp = _mm(og_p.reshape(mp, d), hg_w_o, resid=xp.reshape(mp, d), gate=mods_p[1], gate_col=2,
             gate_group_rows=seq, out_dtype=F32, tm=1024, tn=1024, tk=1024, name="hg_o").reshape(bp, seq, d)

    hs = _prenorm(xs, norm_g4, 2, mods_s[1], 0, tt=bs)
    proj_s = _mm(hs.reshape(bs, d), hg_w_in, out_dtype=F32, tm=bs, tn=1024, tk=2048, name="hg_in_s")
    o_s, st_s = _hgrn_decode(proj_s.reshape(bs, 4, hg_heads, HG_KDIM), lb.reshape(hg_heads, HG_KDIM), state_hgrn[0])
    og_s = _gated_norm(o_s.reshape(1, bs, d), proj_s.reshape(1, bs, 4 * d), hg_g_norm, tt=bs)
    xs = _mm(og_s.reshape(bs, d), hg_w_o, resid=xs.reshape(bs, d), gate=mods_s[1], gate_col=2, out_dtype=F32,
             tm=bs, tn=1024, tk=2048, name="hg_o_s").reshape(1, bs, d)

    router = jnp.concatenate([moe_router[0], jnp.zeros((d, LANE - N_EXPERTS), F32)], axis=1)
    hp, route_p = _prenorm(xp, norm_g4, 3, mods_p[1], 3, router, tt=256)
    hs, route_s = _prenorm(xs, norm_g4, 3, mods_s[1], 3, router, tt=bs)
    h_all = jnp.concatenate([hp.reshape(mp, d), hs.reshape(bs, d)], axis=0)
    route = jnp.concatenate([route_p.reshape(mp, LANE), route_s.reshape(bs, LANE)], axis=0)
    n_tok = mp + bs
    n_pair = n_tok * TOP_K
    tm = MOE_TM
    n_tiles = pl.cdiv(n_pair + N_EXPERTS * (tm - 1), tm)
    r_rows = n_tiles * tm
    e_pair = route[:, :TOP_K].astype(I32).reshape(n_pair)
    onehot = (e_pair[:, None] == jnp.arange(N_EXPERTS, dtype=I32)[None, :]).astype(I32)
    counts = jnp.sum(onehot, axis=0)
    rank = jnp.take_along_axis(jnp.cumsum(onehot, axis=0) - onehot, e_pair[:, None], axis=1)[:, 0]
    padded = ((counts + tm - 1) // tm) * tm
    ends = jnp.cumsum(padded)
    starts = ends - padded
    dest = starts[e_pair] + rank
    src_token = jnp.zeros((r_rows,), I32).at[dest].set(jnp.arange(n_pair, dtype=I32) // TOP_K)
    n_valid = (ends[-1] // tm).astype(I32).reshape(1)
    tile_start = jnp.minimum(jnp.arange(n_tiles, dtype=I32), n_valid[0] - 1) * tm
    tile_expert = jnp.minimum(jnp.searchsorted(ends, tile_start, side="right"), N_EXPERTS - 1).astype(I32)
    x_sorted = jnp.take(h_all, src_token, axis=0)
    e_dim = moe_w1.shape[-1]
    act = _moe_up(tile_expert, n_valid, x_sorted, moe_w1.reshape(N_EXPERTS, d, e_dim),
                  moe_w3.reshape(N_EXPERTS, d, e_dim), tm=tm, tn=512)
    y_sorted = _moe_down(tile_expert, n_valid, act, moe_w2.reshape(N_EXPERTS, e_dim, d), tm=tm, tk=512)
    dest2 = dest.reshape(n_tok, TOP_K).T
    y_tok_p = jnp.take(y_sorted, dest2[:, :mp], axis=0).reshape(TOP_K, bp, seq, d)
    y_tok_s = jnp.take(y_sorted, dest2[:, mp:], axis=0).reshape(TOP_K, 1, bs, d)
    w_tok = jnp.concatenate([route[:, TOP_K:2 * TOP_K], jnp.zeros((n_tok, LANE - TOP_K), F32)], axis=1)
    fn = final_norm.reshape(1, d)
    y_prompt = _moe_combine_final(xp, mods_p[1], 5, y_tok_p, w_tok[:mp].reshape(bp, seq, LANE), fn, tt=256)
    y_sample = _moe_combine_final(xs, mods_s[1], 5, y_tok_s, w_tok[mp:].reshape(1, bs, LANE), fn, tt=bs)

    return (y_prompt, y_sample.reshape(bs, dec_seq, d),
            ckv_p[None], kpe_p[None], st_p[None],
            ckv_s.reshape(1, bs, dec_seq, KV_LORA), kpe_s.reshape(1, bs, dec_seq, ROPE_DIM), st_s[None])
```

```python
import functools

import jax
import jax.numpy as jnp
from jax import lax
from jax.experimental import pallas as pl
from jax.experimental.pallas import tpu as pltpu

F32 = jnp.float32
BF16 = jnp.bfloat16
I32 = jnp.int32

PAGE_SIZE = 128
MLA_HEADS = 32
Q_LORA = 1024
KV_LORA = 512
NOPE_DIM = 128
ROPE_DIM = 64
V_DIM = 128
ROPE_THETA = 10000.0
ATTN_SCALE = (NOPE_DIM + ROPE_DIM) ** -0.5
HG_KDIM = 128
HG_CHUNK = 64
N_EXPERTS = 8
TOP_K = 2
N_ADA = 6
EPS = 1e-6
NEG_INF = -1e30

LANE = 128
MIB = 1024 * 1024
VMEM_CAP_BYTES = 56 * MIB


def _params(n_grid, vmem_mib):
    return pltpu.CompilerParams(
        dimension_semantics=("arbitrary",) * n_grid,
        vmem_limit_bytes=min(vmem_mib * MIB, VMEM_CAP_BYTES))


def _silu(x):
    return x * jax.nn.sigmoid(x)


def _split3(x):
    hi = x.astype(BF16)
    r1 = x - hi.astype(F32)
    mid = r1.astype(BF16)
    lo = (r1 - mid.astype(F32)).astype(BF16)
    return hi, mid, lo


def _mm_kernel(*refs, nk, tk, k_valid, a_silu, n_w, epi):
    it = iter(refs)
    a_ref = next(it)
    w_ref = next(it)
    w3_ref = next(it) if n_w == 2 else None
    bias_ref = next(it) if epi == "bias" else None
    x_ref = next(it) if epi == "resid" else None
    gate_ref = next(it) if epi == "resid" else None
    o_ref = next(it)
    acc_ref = next(it) if nk > 1 else None
    acc3_ref = next(it) if (nk > 1 and n_w == 2) else None

    k = pl.program_id(2)
    ragged = (k_valid % tk) != 0
    w_refs = [w_ref, w3_ref][:n_w]

    def dots(masked):
        a = a_ref[...]
        if a_silu:
            a = _silu(a.astype(F32))
        a = a.astype(BF16)
        if masked:
            col = k * tk + lax.broadcasted_iota(I32, a.shape, 1)
            a = jnp.where(col < k_valid, a, jnp.zeros_like(a))
        out = []
        for ref in w_refs:
            w = ref[...]
            if masked:
                row = k * tk + lax.broadcasted_iota(I32, w.shape, 0)
                w = jnp.where(row < k_valid, w, jnp.zeros_like(w))
            out.append(jnp.dot(a, w.astype(BF16), preferred_element_type=F32))
        return out + [None] * (2 - n_w)

    def epilogue(h, h3):
        if epi == "plain":
            o_ref[...] = h.astype(o_ref.dtype)
        elif epi == "bias":
            o_ref[...] = (h + bias_ref[...]).astype(o_ref.dtype)
        elif epi == "swiglu":
            o_ref[...] = (_silu(h) * h3).astype(o_ref.dtype)
        else:
            o_ref[...] = (x_ref[...] + gate_ref[...] * h).astype(o_ref.dtype)

    if nk == 1:
        epilogue(*dots(ragged))
        return

    @pl.when(k == 0)
    def _():
        d, d3 = dots(False)
        acc_ref[...] = d
        if n_w == 2:
            acc3_ref[...] = d3

    if nk > 2:
        @pl.when((k > 0) & (k < nk - 1))
        def _():
            d, d3 = dots(False)
            acc_ref[...] += d
            if n_w == 2:
                acc3_ref[...] += d3

    @pl.when(k == nk - 1)
    def _():
        d, d3 = dots(ragged)
        epilogue(acc_ref[...] + d, acc3_ref[...] + d3 if n_w == 2 else None)


def _mm(a, w, *, layer=0, w3=None, bias=None, resid=None, gate=None, gate_col=0, gate_group_rows=None,
        out_dtype=F32, tm, tn, tk, a_silu=False, name="mm"):
    m, kdim = a.shape
    n = w.shape[-1]
    tm = min(tm, m)
    tn = min(tn, n)
    tk = min(tk, kdim)
    nk = pl.cdiv(kdim, tk)
    grid = (pl.cdiv(m, tm), pl.cdiv(n, tn), nk)
    n_w = 2 if w3 is not None else 1
    epi = "swiglu" if w3 is not None else "bias" if bias is not None else "resid" if resid is not None else "plain"

    a_bufs = 1 if (nk == 1 and grid[1] > 1) else 2
    a_mode = dict(pipeline_mode=pl.Buffered(1)) if a_bufs == 1 else {}
    in_specs = [pl.BlockSpec((tm, tk), lambda i, j, k: (i, k), **a_mode),
                pl.BlockSpec((None, tk, tn), lambda i, j, k: (layer, k, j))]
    args = [a, w]
    if n_w == 2:
        in_specs.append(pl.BlockSpec((None, tk, tn), lambda i, j, k: (layer, k, j)))
        args.append(w3)
    if epi == "bias":
        in_specs.append(pl.BlockSpec((None, 1, tn), lambda i, j, k: (layer, 0, j)))
        args.append(bias)
    if epi == "resid":
        in_specs.append(pl.BlockSpec((tm, tn), lambda i, j, k: (i, j)))
        args.append(resid)
        r = gate.shape[1]
        nj = n // tn
        if r == 1:
            per = gate_group_rows // tm
            in_specs.append(pl.BlockSpec((None, 1, tn), lambda i, j, k: (i // per, 0, gate_col * nj + j)))
        else:
            in_specs.append(pl.BlockSpec((None, tm, tn), lambda i, j, k: (i, 0, gate_col * nj + j)))
        args.append(gate)

    scratch = []
    if nk > 1:
        scratch = [pltpu.VMEM((tm, tn), F32)] * n_w
    out_b = jnp.dtype(out_dtype).itemsize
    est = (a_bufs * tm * tk * a.dtype.itemsize + 2 * n_w * tk * tn * 4 + n_w * tk * tn * 2
           + (n_w + 1) * tm * tn * 4 + 2 * tm * tn * out_b + (3 * tm * tn * 4 if epi == "resid" else 0))
    kern = functools.partial(_mm_kernel, nk=nk, tk=tk, k_valid=kdim, a_silu=a_silu, n_w=n_w, epi=epi)
    return pl.pallas_call(
        kern,
        out_shape=jax.ShapeDtypeStruct((m, n), out_dtype),
        grid=grid,
        in_specs=in_specs,
        out_specs=pl.BlockSpec((tm, tn), lambda i, j, k: (i, j)),
        scratch_shapes=scratch,
        compiler_params=_params(3, est // MIB + 8),
        name=name,
    )(*args)


def _prenorm_kernel(*refs, with_router):
    if with_router:
        x_ref, g_ref, sh_ref, sc_ref, r_ref, o_ref, route_ref = refs
    else:
        x_ref, g_ref, sh_ref, sc_ref, o_ref = refs
    x = x_ref[...]
    y = x * lax.rsqrt(jnp.mean(x * x, axis=-1, keepdims=True) + EPS) * g_ref[...]
    h = y * (1.0 + sc_ref[...]) + sh_ref[...]
    if not with_router:
        o_ref[...] = h.astype(o_ref.dtype)
    else:
        half = h.shape[-1] // 2
        bits = lax.bitcast_convert_type(h.astype(BF16).astype(F32), jnp.uint32)
        o_ref[...] = bits[:, half:] | (bits[:, :half] >> 16)
        h1, h2, _ = _split3(h)
        r = r_ref[...]
        r1, r2, _ = _split3(r)
        logits = (jnp.dot(h1, r1, preferred_element_type=F32)
                  + jnp.dot(h1, r2, preferred_element_type=F32)
                  + jnp.dot(h2, r1, preferred_element_type=F32))
        lane = lax.broadcasted_iota(I32, logits.shape, 1).astype(F32)
        logits = jnp.where(lane < N_EXPERTS, logits, -jnp.inf)
        v1 = jnp.max(logits, axis=-1, keepdims=True)
        i1 = jnp.min(jnp.where(logits == v1, lane, float(LANE)), axis=-1, keepdims=True)
        rest = jnp.where(lane == i1, -jnp.inf, logits)
        v2 = jnp.max(rest, axis=-1, keepdims=True)
        i2 = jnp.min(jnp.where(rest == v2, lane, float(LANE)), axis=-1, keepdims=True)
        e2 = jnp.exp(v2 - v1)
        w1 = 1.0 / (1.0 + e2)
        w2 = e2 / (1.0 + e2)
        route = jnp.where(lane == 0, i1, jnp.where(lane == 1, i2, jnp.where(lane == 2, w1, jnp.where(lane == 3, w2, 0.0))))
        route_ref[...] = route


def _prenorm(x3, norm_g4, g_idx, mod, shift_col, router=None, *, tt):
    b, t, d = x3.shape
    tt = min(tt, t)
    r = mod.shape[1]
    if r == 1:
        mod_spec = lambda c: pl.BlockSpec((None, 1, d), lambda bi, ti: (bi, 0, c))
    else:
        mod_spec = lambda c: pl.BlockSpec((None, tt, d), lambda bi, ti: (bi, ti, c))
    in_specs = [pl.BlockSpec((None, tt, d), lambda bi, ti: (bi, ti, 0)),
                pl.BlockSpec((None, 1, d), lambda bi, ti: (g_idx, 0, 0)),
                mod_spec(shift_col), mod_spec(shift_col + 1)]
    args = [x3, norm_g4, mod, mod]
    if router is None:
        out_shape = [jax.ShapeDtypeStruct((b, t, d), BF16)]
        out_specs = [pl.BlockSpec((None, tt, d), lambda bi, ti: (bi, ti, 0))]
    else:
        out_shape = [jax.ShapeDtypeStruct((b, t, d // 2), jnp.uint32)]
        out_specs = [pl.BlockSpec((None, tt, d // 2), lambda bi, ti: (bi, ti, 0))]
    if router is not None:
        in_specs.append(pl.BlockSpec((d, LANE), lambda bi, ti: (0, 0)))
        args.append(router)
        out_shape.append(jax.ShapeDtypeStruct((b, t, LANE), F32))
        out_specs.append(pl.BlockSpec((None, tt, LANE), lambda bi, ti: (bi, ti, 0)))
    res = pl.pallas_call(
        functools.partial(_prenorm_kernel, with_router=router is not None),
        out_shape=out_shape, grid=(b, t // tt), in_specs=in_specs, out_specs=out_specs,
        compiler_params=_params(2, 48), name="prenorm_router" if router is not None else "prenorm",
    )(*args)
    return res if router is not None else res[0]


def _mla_post_kernel(p_ref, qn_ref, kn_ref, cos_ref, sin_ref, cq_ref, ckv_ref, kpe_ref):
    cq = p_ref[:, :Q_LORA]
    cq = cq * lax.rsqrt(jnp.mean(cq * cq, axis=-1, keepdims=True) + EPS) * qn_ref[...]
    cq_ref[...] = cq.astype(cq_ref.dtype)
    c = p_ref[:, Q_LORA:Q_LORA + KV_LORA]
    ckv_ref[...] = c * lax.rsqrt(jnp.mean(c * c, axis=-1, keepdims=True) + EPS) * kn_ref[...]
    slab = p_ref[:, Q_LORA + KV_LORA:]
    rot = pltpu.roll(slab, ROPE_DIM, 1)
    kpe = slab * cos_ref[...] + rot * sin_ref[...]
    kpe_ref[...] = kpe[:, :ROPE_DIM]


def _mla_post(proj3, q_norm, kv_norm, cos_p, sin_p, *, tt):
    b, t, n = proj3.shape
    tt = min(tt, t)
    return pl.pallas_call(
        _mla_post_kernel,
        out_shape=[jax.ShapeDtypeStruct((b, t, Q_LORA), BF16),
                   jax.ShapeDtypeStruct((b, t, KV_LORA), F32),
                   jax.ShapeDtypeStruct((b, t, ROPE_DIM), F32)],
        grid=(b, t // tt),
        in_specs=[pl.BlockSpec((None, tt, n), lambda bi, ti: (bi, ti, 0)),
                  pl.BlockSpec((1, Q_LORA), lambda bi, ti: (0, 0)),
                  pl.BlockSpec((1, KV_LORA), lambda bi, ti: (0, 0)),
                  pl.BlockSpec((tt, LANE), lambda bi, ti: (ti, 0)),
                  pl.BlockSpec((tt, LANE), lambda bi, ti: (ti, 0))],
        out_specs=[pl.BlockSpec((None, tt, Q_LORA), lambda bi, ti: (bi, ti, 0)),
                   pl.BlockSpec((None, tt, KV_LORA), lambda bi, ti: (bi, ti, 0)),
                   pl.BlockSpec((None, tt, ROPE_DIM), lambda bi, ti: (bi, ti, 0))],
        compiler_params=_params(2, 32), name="mla_post",
    )(proj3, q_norm, kv_norm, cos_p, sin_p)


def _attn_prompt_kernel(q_ref, cos_ref, sin_ref, kt_ref, v_ref, wuk_ref, wuv_ref, o_ref,
                        qs_ref, m_ref, l_ref, acc_ref, *, tq, tkv):
    qi = pl.program_id(1)
    hn = MLA_HEADS * NOPE_DIM
    hr = MLA_HEADS * ROPE_DIM
    n_pair = MLA_HEADS // 2
    cos_t = jnp.concatenate([cos_ref[...]] * (hr // LANE), axis=1)
    sin_t = jnp.concatenate([sin_ref[...]] * (hr // LANE), axis=1)
    qpe = (q_ref[:, hn:hn + hr].astype(F32) * cos_t
           + q_ref[:, hn + hr:hn + 2 * hr].astype(F32) * sin_t).astype(BF16)
    n_full = (qi * tq) // tkv
    rows = n_pair * tq

    for par in range(2):
        for mi in range(n_pair):
            h = 2 * mi + par
            ql = jnp.dot(q_ref[:, h * NOPE_DIM:(h + 1) * NOPE_DIM], wuk_ref[h], preferred_element_type=F32)
            qs_ref[mi * tq:(mi + 1) * tq, :KV_LORA] = ql.astype(BF16)
            qs_ref[mi * tq:(mi + 1) * tq, KV_LORA:] = qpe[:, mi * LANE:(mi + 1) * LANE]
        m_ref[...] = jnp.full(m_ref.shape, -jnp.inf, F32)
        l_ref[...] = jnp.zeros(l_ref.shape, F32)
        acc_ref[...] = jnp.zeros(acc_ref.shape, F32)

        def step(j, masked):
            s = jnp.dot(qs_ref[...], kt_ref[par, j], preferred_element_type=F32) * ATTN_SCALE
            if masked:
                r = lax.broadcasted_iota(I32, (rows, tkv), 0)
                q_pos = qi * tq + (r & (tq - 1))
                k_pos = j * tkv + lax.broadcasted_iota(I32, (rows, tkv), 1)
                s = jnp.where(k_pos <= q_pos, s, NEG_INF)
            m_prev = m_ref[...]
            m_new = jnp.maximum(m_prev, jnp.max(s, axis=1, keepdims=True))
            alpha = jnp.exp(m_prev - m_new)
            p = jnp.exp(s - jnp.tile(m_new, (1, tkv // LANE)))
            l_ref[...] = alpha * l_ref[...] + jnp.sum(p, axis=1, keepdims=True)
            pv = jnp.dot(p.astype(BF16), v_ref[j], preferred_element_type=F32)
            acc_ref[...] = acc_ref[...] * jnp.tile(alpha, (1, KV_LORA // LANE)) + pv
            m_ref[...] = m_new

        def body(j, carry):
            step(j, False)
            return carry

        lax.fori_loop(0, n_full, body, 0)
        step(n_full, True)

        inv = 1.0 / l_ref[...]
        for mi in range(n_pair):
            h = 2 * mi + par
            sl = slice(mi * tq, (mi + 1) * tq)
            lat = (acc_ref[sl, :] * jnp.tile(inv[sl, :], (1, KV_LORA // LANE))).astype(BF16)
            o_ref[:, h * V_DIM:(h + 1) * V_DIM] = jnp.dot(
                lat, wuv_ref[h], preferred_element_type=F32).astype(o_ref.dtype)


def _attn_prompt(q3, cos128, sin128, kt, v4, wuk, wuv, *, tq, tkv):
    b, t, nq = q3.shape
    nkv = t // tkv
    kc = KV_LORA + LANE
    rows = (MLA_HEADS // 2) * tq
    one = pl.Buffered(1)
    return pl.pallas_call(
        functools.partial(_attn_prompt_kernel, tq=tq, tkv=tkv),
        out_shape=jax.ShapeDtypeStruct((b, t, MLA_HEADS * V_DIM), BF16),
        grid=(b, t // tq),
        in_specs=[pl.BlockSpec((None, tq, nq), lambda bi, qi: (bi, qi, 0)),
                  pl.BlockSpec((tq, LANE), lambda bi, qi: (qi, 0)),
                  pl.BlockSpec((tq, LANE), lambda bi, qi: (qi, 0)),
                  pl.BlockSpec((None, 2, nkv, kc, tkv), lambda bi, qi: (bi, 0, 0, 0, 0), pipeline_mode=one),
                  pl.BlockSpec((None, nkv, tkv, KV_LORA), lambda bi, qi: (bi, 0, 0, 0), pipeline_mode=one),
                  pl.BlockSpec((MLA_HEADS, NOPE_DIM, KV_LORA), lambda bi, qi: (0, 0, 0), pipeline_mode=one),
                  pl.BlockSpec((MLA_HEADS, KV_LORA, V_DIM), lambda bi, qi: (0, 0, 0), pipeline_mode=one)],
        out_specs=pl.BlockSpec((None, tq, MLA_HEADS * V_DIM), lambda bi, qi: (bi, qi, 0)),
        scratch_shapes=[pltpu.VMEM((rows, kc), BF16), pltpu.VMEM((rows, LANE), F32),
                        pltpu.VMEM((rows, LANE), F32), pltpu.VMEM((rows, KV_LORA), F32)],
        compiler_params=_params(2, 56), name="mla_attn_prompt",
    )(q3, cos128, sin128, kt, v4, wuk, wuv)


def _head_mm_kernel(a_ref, w_ref, o_ref):
    o_ref[...] = jnp.dot(a_ref[...].astype(BF16), w_ref[...], preferred_element_type=F32).astype(o_ref.dtype)


def _head_mm(a, w, ka, na):
    m = a.shape[0]
    h = w.shape[0]
    return pl.pallas_call(
        _head_mm_kernel,
        out_shape=jax.ShapeDtypeStruct((m, h * na), BF16),
        grid=(h,),
        in_specs=[pl.BlockSpec((m, ka), lambda hi: (0, hi)),
                  pl.BlockSpec((None, ka, na), lambda hi: (hi, 0, 0))],
        out_specs=pl.BlockSpec((m, na), lambda hi: (0, hi)),
        compiler_params=_params(1, 16), name="head_mm",
    )(a, w)


def _rope_q_kernel(x_ref, xr_ref, cos_ref, sin_ref, o_ref):
    o_ref[...] = (x_ref[...].astype(F32) * cos_ref[...] + xr_ref[...].astype(F32) * sin_ref[...]).astype(o_ref.dtype)


def _rope_q(q, cos_t, sin_t):
    m = q.shape[0]
    hr = MLA_HEADS * ROPE_DIM
    nope_blocks = (MLA_HEADS * NOPE_DIM) // hr
    return pl.pallas_call(
        _rope_q_kernel,
        out_shape=jax.ShapeDtypeStruct((m, hr), BF16),
        grid=(1,),
        in_specs=[pl.BlockSpec((m, hr), lambda i: (0, nope_blocks)),
                  pl.BlockSpec((m, hr), lambda i: (0, nope_blocks + 1)),
                  pl.BlockSpec((m, hr), lambda i: (0, 0)),
                  pl.BlockSpec((m, hr), lambda i: (0, 0))],
        out_specs=pl.BlockSpec((m, hr), lambda i: (0, 0)),
        compiler_params=_params(1, 16), name="rope_q",
    )(q, q, cos_t, sin_t)


def _attn_decode_kernel(pt_ref, ql_ref, qp_ref, cn_ref, kn_ref, *rest, ns, pps, n_steps):
    n_pg = ns * pps
    ckv_refs = rest[:n_pg]
    kpe_refs = rest[n_pg:2 * n_pg]
    o_ref = rest[2 * n_pg]
    m_ref, l_ref, acc_ref, kbuf_ref, pbuf_ref = rest[2 * n_pg + 1:]
    c = pl.program_id(1)
    nt = (((1,), (1,)), ((), ()))

    for si in range(ns):
        for p in range(pps):
            kbuf_ref[si, p * PAGE_SIZE:(p + 1) * PAGE_SIZE, :] = ckv_refs[si * pps + p][...].astype(BF16)
            pbuf_ref[si, :, p * PAGE_SIZE:(p + 1) * PAGE_SIZE] = kpe_refs[si * pps + p][...].astype(BF16)

    @pl.when(c == 0)
    def _():
        for si in range(ns):
            cn = cn_ref[si]
            kn = kn_ref[si]
            s_new = (jnp.sum(ql_ref[si].astype(F32) * cn, axis=-1, keepdims=True)
                     + jnp.sum(qp_ref[si].astype(F32) * kn, axis=-1, keepdims=True)) * ATTN_SCALE
            m_ref[si] = jnp.broadcast_to(s_new, m_ref.shape[1:])
            l_ref[si] = jnp.ones(l_ref.shape[1:], F32)
            acc_ref[si] = jnp.broadcast_to(cn, acc_ref.shape[1:])

    for si in range(ns):
        kb = kbuf_ref[si]
        s = (lax.dot_general(ql_ref[si], kb, nt, preferred_element_type=F32)
             + jnp.dot(qp_ref[si], pbuf_ref[si], preferred_element_type=F32)) * ATTN_SCALE
        m_prev = m_ref[si]
        m_new = jnp.maximum(m_prev, jnp.max(s, axis=1, keepdims=True))
        alpha = jnp.exp(m_prev - m_new)
        p_all = jnp.exp(s - jnp.tile(m_new, (1, pps))).astype(BF16)
        l_ref[si] = alpha * l_ref[si] + jnp.sum(p_all.astype(F32), axis=1, keepdims=True)
        pv = jnp.dot(p_all, kb, preferred_element_type=F32)
        acc_ref[si] = acc_ref[si] * jnp.tile(alpha, (1, KV_LORA // LANE)) + pv
        m_ref[si] = m_new

    @pl.when(c == n_steps - 1)
    def _():
        for si in range(ns):
            o_ref[si] = (acc_ref[si] / jnp.tile(l_ref[si], (1, KV_LORA // LANE))).astype(o_ref.dtype)


def _attn_decode(page_table, ql3, qp3, ckv_new3, kpe_new3, cache_ckv, cache_kpe_t, layer, *, ns, pps):
    nb, n_pages = page_table.shape
    n_steps = n_pages // pps
    h = MLA_HEADS

    def page_spec(rows, width, si, p):
        return pl.BlockSpec((None, None, rows, width),
                            lambda bi, ci, pt: (layer, pt[bi * ns + si, ci * pps + p], 0, 0))

    in_specs = [pl.BlockSpec((ns, h, KV_LORA), lambda bi, ci, pt: (bi, 0, 0)),
                pl.BlockSpec((ns, h, ROPE_DIM), lambda bi, ci, pt: (bi, 0, 0)),
                pl.BlockSpec((ns, 1, KV_LORA), lambda bi, ci, pt: (bi, 0, 0)),
                pl.BlockSpec((ns, 1, ROPE_DIM), lambda bi, ci, pt: (bi, 0, 0))]
    in_specs += [page_spec(PAGE_SIZE, KV_LORA, si, p) for si in range(ns) for p in range(pps)]
    in_specs += [page_spec(ROPE_DIM, PAGE_SIZE, si, p) for si in range(ns) for p in range(pps)]
    grid_spec = pltpu.PrefetchScalarGridSpec(
        num_scalar_prefetch=1, grid=(nb // ns, n_steps), in_specs=in_specs,
        out_specs=pl.BlockSpec((ns, h, KV_LORA), lambda bi, ci, pt: (bi, 0, 0)),
        scratch_shapes=[pltpu.VMEM((ns, h, LANE), F32), pltpu.VMEM((ns, h, LANE), F32),
                        pltpu.VMEM((ns, h, KV_LORA), F32),
                        pltpu.VMEM((ns, pps * PAGE_SIZE, KV_LORA), BF16),
                        pltpu.VMEM((ns, ROPE_DIM, pps * PAGE_SIZE), BF16)])
    n_pg = ns * pps
    return pl.pallas_call(
        functools.partial(_attn_decode_kernel, ns=ns, pps=pps, n_steps=n_steps),
        out_shape=jax.ShapeDtypeStruct((nb, h, KV_LORA), BF16),
        grid_spec=grid_spec,
        compiler_params=_params(2, 40), name="mla_attn_decode",
    )(page_table, ql3, qp3, ckv_new3, kpe_new3, *([cache_ckv] * n_pg), *([cache_kpe_t] * n_pg))


def _hgrn_prompt_kernel(q_ref, f_ref, i_ref, lb_ref, o_ref, s_out_ref, s_ref, *, tt, hb, n_tb):
    tb = pl.program_id(2)

    @pl.when(tb == 0)
    def _():
        s_ref[...] = jnp.zeros(s_ref.shape, F32)

    c = HG_CHUNK
    dk = HG_KDIM
    lb = lb_ref[...]
    sg = jax.nn.sigmoid(f_ref[...])
    log_f = jnp.log(lb + (1.0 - lb) * sg)
    kk = (1.0 - lb) * (1.0 - sg)
    qs = _silu(q_ref[...])
    vv = i_ref[...]
    row = lax.broadcasted_iota(I32, (c, c), 0)
    col = lax.broadcasted_iota(I32, (c, c), 1)
    causal = col <= row
    tril = causal.astype(BF16)
    mid = (c - 1) // 2
    nt = (((1,), (1,)), ((), ()))

    for ci in range(tt // c):
        rs = slice(ci * c, (ci + 1) * c)
        lf1, lf2, lf3 = _split3(log_f[rs])
        cum = (jnp.dot(tril, lf1, preferred_element_type=F32)
               + jnp.dot(tril, lf2, preferred_element_type=F32)
               + jnp.dot(tril, lf3, preferred_element_type=F32))
        ref = cum[mid:mid + 1]
        last = cum[c - 1:c]
        qc = qs[rs]
        kc = kk[rs]
        vb = vv[rs].astype(BF16)
        qd = (qc * jnp.exp(cum - ref)).astype(BF16)
        kd = (kc * jnp.exp(ref - cum)).astype(BF16)
        q0 = (qc * jnp.exp(cum)).astype(BF16)
        k2 = kc * jnp.exp(last - cum)
        dec = jnp.exp(last)
        for h in range(hb):
            ls = slice(h * dk, (h + 1) * dk)
            a = lax.dot_general(qd[:, ls], kd[:, ls], nt, preferred_element_type=F32)
            a = jnp.where(causal, a, 0.0).astype(BF16)
            s_prev = s_ref[h]
            o = (jnp.dot(q0[:, ls], s_prev.astype(BF16), preferred_element_type=F32)
                 + jnp.dot(a, vb[:, ls], preferred_element_type=F32))
            o_ref[rs, ls] = o
            dec_col = jnp.broadcast_to(dec[:, ls], (dk, dk)).T
            k2t = k2[:, ls].T.astype(BF16)
            s_ref[h] = dec_col * s_prev + jnp.dot(k2t, vb[:, ls], preferred_element_type=F32)

    @pl.when(tb == n_tb - 1)
    def _():
        s_out_ref[...] = s_ref[...]


def _hgrn_prompt(proj3, lb2, *, tt, hb):
    b, t, n4 = proj3.shape
    d = n4 // 4
    heads = d // HG_KDIM
    w = hb * HG_KDIM
    nhg = heads // hb
    n_tb = t // tt
    return pl.pallas_call(
        functools.partial(_hgrn_prompt_kernel, tt=tt, hb=hb, n_tb=n_tb),
        out_shape=[jax.ShapeDtypeStruct((b, t, d), F32),
                   jax.ShapeDtypeStruct((b, heads, HG_KDIM, HG_KDIM), F32)],
        grid=(b, nhg, n_tb),
        in_specs=[pl.BlockSpec((None, tt, w), lambda bi, hi, ti: (bi, ti, hi)),
                  pl.BlockSpec((None, tt, w), lambda bi, hi, ti: (bi, ti, nhg + hi)),
                  pl.BlockSpec((None, tt, w), lambda bi, hi, ti: (bi, ti, 2 * nhg + hi)),
                  pl.BlockSpec((1, w), lambda bi, hi, ti: (0, hi))],
        out_specs=[pl.BlockSpec((None, tt, w), lambda bi, hi, ti: (bi, ti, hi)),
                   pl.BlockSpec((None, hb, HG_KDIM, HG_KDIM), lambda bi, hi, ti: (bi, hi, 0, 0))],
        scratch_shapes=[pltpu.VMEM((hb, HG_KDIM, HG_KDIM), F32)],
        compiler_params=_params(3, 32), name="hgrn_prompt",
    )(proj3, proj3, proj3, lb2)


def _hgrn_decode_kernel(p_ref, lb_ref, s_ref, o_ref, s_out_ref):
    heads = s_ref.shape[0]
    dk = HG_KDIM
    lb = lb_ref[...]
    sg = jax.nn.sigmoid(p_ref[1])
    fg = lb + (1.0 - lb) * sg
    kk = (1.0 - lb) * (1.0 - sg)
    qs = _silu(p_ref[0])
    vv = p_ref[2]
    outs = []
    for h in range(heads):
        f_col = jnp.broadcast_to(fg[h:h + 1], (dk, dk)).T
        k_col = jnp.broadcast_to(kk[h:h + 1], (dk, dk)).T
        q_col = jnp.broadcast_to(qs[h:h + 1], (dk, dk)).T
        s_new = f_col * s_ref[h] + k_col * vv[h:h + 1]
        s_out_ref[h] = s_new
        outs.append(jnp.sum(q_col * s_new, axis=0, keepdims=True))
    o_ref[...] = jnp.concatenate(outs, axis=0)


def _hgrn_decode(proj4, lb2, state):
    b, _, heads, dk = proj4.shape
    return pl.pallas_call(
        _hgrn_decode_kernel,
        out_shape=[jax.ShapeDtypeStruct((b, heads, dk), F32),
                   jax.ShapeDtypeStruct(state.shape, F32)],
        grid=(b,),
        in_specs=[pl.BlockSpec((None, 4, heads, dk), lambda bi: (bi, 0, 0, 0)),
                  pl.BlockSpec((heads, dk), lambda bi: (0, 0)),
                  pl.BlockSpec((None, heads, dk, dk), lambda bi: (bi, 0, 0, 0))],
        out_specs=[pl.BlockSpec((None, heads, dk), lambda bi: (bi, 0, 0)),
                   pl.BlockSpec((None, heads, dk, dk), lambda bi: (bi, 0, 0, 0))],
        compiler_params=_params(1, 32), name="hgrn_decode",
    )(proj4, lb2, state)


def _gated_norm_kernel(o_ref, g_ref, gn_ref, out_ref):
    o = o_ref[...]
    y = o * lax.rsqrt(jnp.mean(o * o, axis=-1, keepdims=True) + EPS) * gn_ref[...]
    out_ref[...] = (y * _silu(g_ref[...])).astype(out_ref.dtype)


def _gated_norm(o3, proj3, g_norm, *, tt):
    b, t, d = o3.shape
    tt = min(tt, t)
    return pl.pallas_call(
        _gated_norm_kernel,
        out_shape=jax.ShapeDtypeStruct((b, t, d), BF16),
        grid=(b, t // tt),
        in_specs=[pl.BlockSpec((None, tt, d), lambda bi, ti: (bi, ti, 0)),
                  pl.BlockSpec((None, tt, d), lambda bi, ti: (bi, ti, 3)),
                  pl.BlockSpec((1, d), lambda bi, ti: (0, 0))],
        out_specs=pl.BlockSpec((None, tt, d), lambda bi, ti: (bi, ti, 0)),
        compiler_params=_params(2, 48), name="gated_norm",
    )(o3, proj3, g_norm)


def _moe_up_kernel(hl_ref, he_ref, nv_ref, a_ref, w1_ref, w3_ref, o_ref):
    i = pl.program_id(1)

    @pl.when(i < nv_ref[0])
    def _():
        words = a_ref[...]
        half = words.shape[1]
        a_lo = lax.bitcast_convert_type(words << 16, F32).astype(BF16)
        a_hi = lax.bitcast_convert_type(words & jnp.uint32(0xFFFF0000), F32).astype(BF16)

        def proj(w_ref):
            w = w_ref[...].astype(BF16)
            return (jnp.dot(a_lo, w[:half], preferred_element_type=F32)
                    + jnp.dot(a_hi, w[half:], preferred_element_type=F32))

        o_ref[...] = (_silu(proj(w1_ref)) * proj(w3_ref)).astype(o_ref.dtype)


def _moe_up(half_list, half_expert, n_occupied, xs, w1, w3, *, tm, tn):
    r, dh = xs.shape
    d = 2 * dh
    n = w1.shape[-1]
    tn = min(tn, n)
    grid_spec = pltpu.PrefetchScalarGridSpec(
        num_scalar_prefetch=3, grid=(n // tn, r // tm),
        in_specs=[pl.BlockSpec((tm, dh), lambda j, i, hl, he, nv: (hl[i], 0)),
                  pl.BlockSpec((None, d, tn), lambda j, i, hl, he, nv: (he[i], 0, j)),
                  pl.BlockSpec((None, d, tn), lambda j, i, hl, he, nv: (he[i], 0, j))],
        out_specs=pl.BlockSpec((tm, tn), lambda j, i, hl, he, nv: (hl[i], j)))
    est = 2 * tm * dh * 4 + 2 * tm * d * 2 + 4 * d * tn * 4 + 2 * d * tn * 2 + 2 * tm * tn * 2 + 3 * tm * tn * 4
    return pl.pallas_call(
        _moe_up_kernel,
        out_shape=jax.ShapeDtypeStruct((r, n), BF16),
        grid_spec=grid_spec,
        compiler_params=_params(2, est // MIB + 4), name="moe_up",
    )(half_list, half_expert, n_occupied, xs, w1, w3)


def _moe_down_kernel(tl_ref, te_ref, sv_ref, nv_ref, a_ref, w_ref, o_ref, *, half):
    i = pl.program_id(0)
    k = pl.program_id(2)
    valid = i < nv_ref[0]
    both = valid & (sv_ref[i] != 0)
    first_only = valid & (sv_ref[i] == 0)

    @pl.when(both & (k == 0))
    def _():
        o_ref[...] = jnp.dot(a_ref[...], w_ref[...].astype(BF16), preferred_element_type=F32)

    @pl.when(both & (k > 0))
    def _():
        o_ref[...] += jnp.dot(a_ref[...], w_ref[...].astype(BF16), preferred_element_type=F32)

    @pl.when(first_only & (k == 0))
    def _():
        o_ref[:half, :] = jnp.dot(a_ref[:half, :], w_ref[...].astype(BF16), preferred_element_type=F32)
        o_ref[half:, :] = jnp.zeros((o_ref.shape[0] - half, o_ref.shape[1]), o_ref.dtype)

    @pl.when(first_only & (k > 0))
    def _():
        o_ref[:half, :] += jnp.dot(a_ref[:half, :], w_ref[...].astype(BF16), preferred_element_type=F32)


def _moe_down(tile_list, tile_expert, second_valid, n_occupied, act, w2, *, tm, tn, tk):
    r, kdim = act.shape
    n = w2.shape[-1]
    tn = min(tn, n)
    tk = min(tk, kdim)
    nk = kdim // tk
    nn = n // tn

    def kk(i, k, nv):
        return jnp.where(i < nv[0], k, nk - 1)

    def jj(i, j, nv):
        return jnp.where(i < nv[0], j, nn - 1)

    grid_spec = pltpu.PrefetchScalarGridSpec(
        num_scalar_prefetch=4, grid=(r // tm, nn, nk),
        in_specs=[pl.BlockSpec((tm, tk), lambda i, j, k, tl, te, sv, nv: (tl[i], kk(i, k, nv))),
                  pl.BlockSpec((None, tk, tn), lambda i, j, k, tl, te, sv, nv: (te[i], kk(i, k, nv), jj(i, j, nv)))],
        out_specs=pl.BlockSpec((tm, tn), lambda i, j, k, tl, te, sv, nv: (tl[i], jj(i, j, nv))))
    est = 2 * tm * tk * 2 + 2 * tk * tn * 4 + tk * tn * 2 + 2 * tm * tn * 4 + tm * tn * 4
    return pl.pallas_call(
        functools.partial(_moe_down_kernel, half=tm // 2),
        out_shape=jax.ShapeDtypeStruct((r, n), F32),
        grid_spec=grid_spec,
        compiler_params=_params(3, est // MIB + 6), name="moe_down",
    )(tile_list, tile_expert, second_valid, n_occupied, act, w2)


def _moe_combine_kernel(x_ref, gate_ref, y_ref, w_ref, fn_ref, o_ref):
    w = w_ref[...]
    y = w[:, 0:1] * y_ref[0] + w[:, 1:2] * y_ref[1]
    x = x_ref[...] + gate_ref[...] * y
    o_ref[...] = x * lax.rsqrt(jnp.mean(x * x, axis=-1, keepdims=True) + EPS) * fn_ref[...]


def _moe_combine_final(x3, mod, gate_col, y4, w3, final_norm, *, tt):
    b, t, d = x3.shape
    tt = min(tt, t)
    r = mod.shape[1]
    if r == 1:
        gate_spec = pl.BlockSpec((None, 1, d), lambda bi, ti: (bi, 0, gate_col))
    else:
        gate_spec = pl.BlockSpec((None, tt, d), lambda bi, ti: (bi, ti, gate_col))
    return pl.pallas_call(
        _moe_combine_kernel,
        out_shape=jax.ShapeDtypeStruct((b, t, d), F32),
        grid=(b, t // tt),
        in_specs=[pl.BlockSpec((None, tt, d), lambda bi, ti: (bi, ti, 0)),
                  gate_spec,
                  pl.BlockSpec((TOP_K, None, tt, d), lambda bi, ti: (0, bi, ti, 0)),
                  pl.BlockSpec((None, tt, LANE), lambda bi, ti: (bi, ti, 0)),
                  pl.BlockSpec((1, d), lambda bi, ti: (0, 0))],
        out_specs=pl.BlockSpec((None, tt, d), lambda bi, ti: (bi, ti, 0)),
        compiler_params=_params(2, 48), name="moe_combine_final",
    )(x3, mod, y4, w3, final_norm)


MOE_HALF = 512


def kernel(x_prompt, x_sample, cache_ckv, cache_kpe, state_hgrn, page_table, c_prompt, c_sample,
           ada_w, ada_b, norm_g, mla_w_in, mla_q_norm, mla_w_uq, mla_kv_norm, mla_w_uk, mla_w_uv,
           mla_w_o, hg_w_in, hg_lower_bounds, hg_g_norm, hg_w_o, ffn_w1, ffn_w3, ffn_w2,
           moe_router, moe_w1, moe_w3, moe_w2, final_norm):
    bp, seq, d = x_prompt.shape
    bs, dec_seq, _ = x_sample.shape
    depth = ada_w.shape[0]
    assert depth == 2 and dec_seq == 1
    mp = bp * seq
    past_len = page_table.shape[1] * PAGE_SIZE
    heads = MLA_HEADS
    hg_heads = d // HG_KDIM

    n_c = bp + bs
    pad_c = (-n_c) % 8
    c_all = jnp.concatenate([c_prompt, c_sample, jnp.zeros((pad_c, d), F32)], axis=0)
    ada_b3 = ada_b.reshape(depth, 1, N_ADA * d)
    mods_p, mods_s = [], []
    for i in range(depth):
        mod = _mm(c_all, ada_w, layer=i, bias=ada_b3, out_dtype=F32, tm=n_c + pad_c, tn=1024, tk=2048,
                  a_silu=True, name="ada_mod")
        mods_p.append(mod[:bp].reshape(bp, 1, N_ADA * d))
        mods_s.append(mod[bp:bp + bs].reshape(1, bs, N_ADA * d))

    norm_g4 = norm_g.reshape(depth * 2, 1, d)
    xp = x_prompt
    xs = x_sample.reshape(1, bs, d)

    inv = ROPE_THETA ** (-jnp.arange(0, ROPE_DIM, 2, dtype=F32) / ROPE_DIM)

    def tables(pos):
        ang = pos.astype(F32)[:, None] * inv[None, :]
        cos2 = jnp.concatenate([jnp.cos(ang)] * 2, axis=1)
        sin2 = jnp.concatenate([jnp.sin(ang)] * 2, axis=1)
        return cos2, sin2

    cos_p, sin_p = tables(jnp.arange(seq))
    cos_s, sin_s = tables(jnp.broadcast_to(past_len + jnp.arange(dec_seq), (bs,)))
    zpad = lambda a: jnp.concatenate([a, jnp.zeros_like(a)], axis=1)
    dup = lambda a: jnp.concatenate([a, a], axis=1)

    w_in = mla_w_in[0]
    kpe_w = w_in[:, Q_LORA + KV_LORA:]
    half = ROPE_DIM // 2
    w_in_aug = jnp.concatenate([w_in, -kpe_w[:, half:], kpe_w[:, :half]], axis=1)[None]
    wq = mla_w_uq[0].reshape(Q_LORA, heads, NOPE_DIM + ROPE_DIM)
    wq_rope = wq[:, :, NOPE_DIM:]
    wq_rot = jnp.concatenate([-wq_rope[..., half:], wq_rope[..., :half]], axis=-1)
    w_uq_aug = jnp.concatenate([wq[:, :, :NOPE_DIM].reshape(Q_LORA, -1), wq_rope.reshape(Q_LORA, -1),
                                wq_rot.reshape(Q_LORA, -1)], axis=1)[None]
    wuk = jnp.transpose(mla_w_uk[0], (1, 2, 0)).astype(BF16)
    wuv = jnp.transpose(mla_w_uv[0], (1, 0, 2)).astype(BF16)
    q_norm = mla_q_norm[0].reshape(1, Q_LORA)
    kv_norm = mla_kv_norm[0].reshape(1, KV_LORA)

    hp = _prenorm(xp, norm_g4, 0, mods_p[0], 0, tt=256)
    proj_p = _mm(hp.reshape(mp, d), w_in_aug, out_dtype=F32, tm=1024, tn=2048, tk=1024, name="mla_in")
    cq_p, ckv_p, kpe_p = _mla_post(proj_p.reshape(bp, seq, -1), q_norm, kv_norm, zpad(cos_p), zpad(sin_p), tt=512)
    q_p = _mm(cq_p.reshape(mp, Q_LORA), w_uq_aug, out_dtype=BF16, tm=2048, tn=1024, tk=1024, name="mla_uq")
    tkv = 512
    nkv = seq // tkv
    ckv_b = ckv_p.astype(BF16)
    kpe_b = kpe_p.astype(BF16)
    zk = jnp.zeros_like(kpe_b)
    kcat = jnp.stack([jnp.concatenate([ckv_b, kpe_b, zk], axis=-1),
                      jnp.concatenate([ckv_b, zk, kpe_b], axis=-1)], axis=1)
    kt = jnp.swapaxes(kcat.reshape(bp, 2, nkv, tkv, KV_LORA + LANE), -1, -2)
    v4 = ckv_b.reshape(bp, nkv, tkv, KV_LORA)
    o_p = _attn_prompt(q_p.reshape(bp, seq, -1), dup(cos_p), dup(sin_p), kt, v4, wuk, wuv, tq=128, tkv=tkv)
    xp = _mm(o_p.reshape(mp, d), mla_w_o, resid=xp.reshape(mp, d), gate=mods_p[0], gate_col=2,
             gate_group_rows=seq, out_dtype=F32, tm=1024, tn=512, tk=d, name="mla_o").reshape(bp, seq, d)

    hs = _prenorm(xs, norm_g4, 0, mods_s[0], 0, tt=bs)
    proj_s = _mm(hs.reshape(bs, d), w_in_aug, out_dtype=F32, tm=bs, tn=2048, tk=2048, name="mla_in_s")
    cq_s, ckv_s, kpe_s = _mla_post(proj_s.reshape(1, bs, -1), q_norm, kv_norm, zpad(cos_s), zpad(sin_s), tt=bs)
    q_s = _mm(cq_s.reshape(bs, Q_LORA), w_uq_aug, out_dtype=BF16, tm=bs, tn=2048, tk=1024, name="mla_uq_s")
    ql_s = _head_mm(q_s, wuk, NOPE_DIM, KV_LORA).reshape(bs, heads, KV_LORA)
    qp_s = _rope_q(q_s, jnp.tile(cos_s, (1, heads)), jnp.tile(sin_s, (1, heads))).reshape(bs, heads, ROPE_DIM)
    lat_s = _attn_decode(page_table, ql_s, qp_s, ckv_s.reshape(bs, 1, KV_LORA), kpe_s.reshape(bs, 1, ROPE_DIM),
                         cache_ckv, jnp.swapaxes(cache_kpe, 2, 3), 0, ns=2, pps=16)
    o_s = _head_mm(lat_s.reshape(bs, heads * KV_LORA), wuv, KV_LORA, V_DIM)
    xs = _mm(o_s, mla_w_o, resid=xs.reshape(bs, d), gate=mods_s[0], gate_col=2, out_dtype=F32,
             tm=bs, tn=1024, tk=2048, name="mla_o_s").reshape(1, bs, d)

    hp = _prenorm(xp, norm_g4, 1, mods_p[0], 3, tt=256)
    act_p = _mm(hp.reshape(mp, d), ffn_w1, w3=ffn_w3, out_dtype=BF16, tm=2048, tn=256, tk=d, name="ffn_up")
    xp = _mm(act_p, ffn_w2, resid=xp.reshape(mp, d), gate=mods_p[0], gate_col=5, gate_group_rows=seq,
             out_dtype=F32, tm=1024, tn=1024, tk=1024, name="ffn_down").reshape(bp, seq, d)
    hs = _prenorm(xs, norm_g4, 1, mods_s[0], 3, tt=bs)
    act_s = _mm(hs.reshape(bs, d), ffn_w1, w3=ffn_w3, out_dtype=BF16, tm=bs, tn=1024, tk=1024, name="ffn_up_s")
    xs = _mm(act_s, ffn_w2, resid=xs.reshape(bs, d), gate=mods_s[0], gate_col=5, out_dtype=F32,
             tm=bs, tn=1024, tk=2048, name="ffn_down_s").reshape(1, bs, d)

    p_lb = jax.nn.softmax(hg_lower_bounds.astype(F32), axis=0)
    lb = (jnp.cumsum(p_lb, axis=0) - p_lb[0:1])[1].reshape(1, d)

    hp = _prenorm(xp, norm_g4, 2, mods_p[1], 0, tt=256)
    proj_p = _mm(hp.reshape(mp, d), hg_w_in, out_dtype=F32, tm=2048, tn=512, tk=d, name="hg_in").reshape(bp, seq, 4 * d)
    o_p, st_p = _hgrn_prompt(proj_p, lb, tt=256, hb=4)
    og_p = _gated_norm(o_p, proj_p, hg_g_norm, tt=256)
    xp = _mm(og_p.reshape(mp, d), hg_w_o, resid=xp.reshape(mp, d), gate=mods_p[1], gate_col=2,
             gate_group_rows=seq, out_dtype=F32, tm=1024, tn=512, tk=d, name="hg_o").reshape(bp, seq, d)

    hs = _prenorm(xs, norm_g4, 2, mods_s[1], 0, tt=bs)
    proj_s = _mm(hs.reshape(bs, d), hg_w_in, out_dtype=F32, tm=bs, tn=1024, tk=2048, name="hg_in_s")
    o_s, st_s = _hgrn_decode(proj_s.reshape(bs, 4, hg_heads, HG_KDIM), lb.reshape(hg_heads, HG_KDIM), state_hgrn[0])
    og_s = _gated_norm(o_s.reshape(1, bs, d), proj_s.reshape(1, bs, 4 * d), hg_g_norm, tt=bs)
    xs = _mm(og_s.reshape(bs, d), hg_w_o, resid=xs.reshape(bs, d), gate=mods_s[1], gate_col=2, out_dtype=F32,
             tm=bs, tn=1024, tk=2048, name="hg_o_s").reshape(1, bs, d)

    router = jnp.concatenate([moe_router[0], jnp.zeros((d, LANE - N_EXPERTS), F32)], axis=1)
    hp, route_p = _prenorm(xp, norm_g4, 3, mods_p[1], 3, router, tt=256)
    hs, route_s = _prenorm(xs, norm_g4, 3, mods_s[1], 3, router, tt=bs)
    h_all = jnp.concatenate([hp.reshape(mp, d // 2), hs.reshape(bs, d // 2)], axis=0)
    route = jnp.concatenate([route_p.reshape(mp, LANE), route_s.reshape(bs, LANE)], axis=0)
    n_tok = mp + bs
    y_sorted, dest2 = _experts(h_all, route, moe_w1[0], moe_w3[0], moe_w2[0], half=MOE_HALF, tn_up=512,
                               tn_down=2048, tk_down=512)
    y_tok_p = y_sorted.at[dest2[:, :mp]].get(mode="promise_in_bounds").reshape(TOP_K, bp, seq, d)
    y_tok_s = y_sorted.at[dest2[:, mp:]].get(mode="promise_in_bounds").reshape(TOP_K, 1, bs, d)
    w_tok = jnp.concatenate([route[:, TOP_K:2 * TOP_K], jnp.zeros((n_tok, LANE - TOP_K), F32)], axis=1)
    fn = final_norm.reshape(1, d)
    y_prompt = _moe_combine_final(xp, mods_p[1], 5, y_tok_p, w_tok[:mp].reshape(bp, seq, LANE), fn, tt=256)
    y_sample = _moe_combine_final(xs, mods_s[1], 5, y_tok_s, w_tok[mp:].reshape(1, bs, LANE), fn, tt=bs)

    return (y_prompt, y_sample.reshape(bs, dec_seq, d),
            ckv_p[None], kpe_p[None], st_p[None],
            ckv_s.reshape(1, bs, dec_seq, KV_LORA), kpe_s.reshape(1, bs, dec_seq, ROPE_DIM), st_s[None])


def _experts(h_all, route, w1, w3, w2, *, half, tn_up, tn_down, tk_down):
    n_tok = h_all.shape[0]
    n_pair = n_tok * TOP_K
    tile = 2 * half
    n_tile = pl.cdiv(n_pair + N_EXPERTS * (tile - 1), tile)
    n_half = 2 * n_tile
    r_rows = n_tile * tile
    e_pair = route[:, :TOP_K].astype(I32).reshape(n_pair)
    onehot = (e_pair[:, None] == jnp.arange(N_EXPERTS, dtype=I32)[None, :]).astype(I32)
    counts = jnp.sum(onehot, axis=0)
    rank = jnp.sum(onehot * (jnp.cumsum(onehot, axis=0) - onehot), axis=1)
    padded = ((counts + tile - 1) // tile) * tile
    ends = jnp.cumsum(padded)
    starts = ends - padded
    dest = jnp.sum(onehot * starts[None, :], axis=1) + rank
    src_token = jnp.zeros((r_rows,), I32).at[dest].set(jnp.arange(n_pair, dtype=I32) // TOP_K)
    half_start = jnp.arange(n_half, dtype=I32) * half
    half_e = jnp.minimum(jnp.sum((ends[None, :] <= half_start[:, None]).astype(I32), axis=1), N_EXPERTS - 1)
    half_oh = (half_e[:, None] == jnp.arange(N_EXPERTS, dtype=I32)[None, :]).astype(I32)
    half_occ = half_start < jnp.sum(half_oh * (starts + counts)[None, :], axis=1)

    def occupied_list(occ):
        n = occ.shape[0]
        occ_i = occ.astype(I32)
        n_occ = jnp.sum(occ_i)
        pos = jnp.where(occ, jnp.cumsum(occ_i) - 1, n)
        lst = jnp.zeros((n + 1,), I32).at[pos].set(jnp.arange(n, dtype=I32))[:n]
        return lst[jnp.minimum(jnp.arange(n, dtype=I32), n_occ - 1)], n_occ.reshape(1)

    half_list, n_half_occ = occupied_list(half_occ)
    tile_list, n_tile_occ = occupied_list(half_occ[0::2])
    x_sorted = h_all.at[src_token].get(mode="promise_in_bounds")
    act = _moe_up(half_list, half_e[half_list], n_half_occ, x_sorted, w1, w3, tm=half, tn=tn_up)
    y_sorted = _moe_down(tile_list, half_e[2 * tile_list], half_occ[2 * tile_list + 1].astype(I32), n_tile_occ,
                         act, w2, tm=tile, tn=tn_down, tk=tk_down)
    return y_sorted, dest.reshape(n_tok, TOP_K).T
```

```python
import functools

import jax
import jax.numpy as jnp
from jax import lax
from jax.experimental import pallas as pl
from jax.experimental.pallas import tpu as pltpu

F32 = jnp.float32
BF16 = jnp.bfloat16
I32 = jnp.int32

PAGE_SIZE = 128
MLA_HEADS = 32
Q_LORA = 1024
KV_LORA = 512
NOPE_DIM = 128
ROPE_DIM = 64
V_DIM = 128
ROPE_THETA = 10000.0
ATTN_SCALE = (NOPE_DIM + ROPE_DIM) ** -0.5
HG_KDIM = 128
HG_CHUNK = 64
N_EXPERTS = 8
TOP_K = 2
N_ADA = 6
EPS = 1e-6
NEG_INF = -1e30

LANE = 128
MIB = 1024 * 1024
VMEM_CAP_BYTES = 56 * MIB


def _params(n_grid, vmem_mib):
    return pltpu.CompilerParams(
        dimension_semantics=("arbitrary",) * n_grid,
        vmem_limit_bytes=min(vmem_mib * MIB, VMEM_CAP_BYTES))


def _silu(x):
    return x * jax.nn.sigmoid(x)


def _split3(x):
    hi = x.astype(BF16)
    r1 = x - hi.astype(F32)
    mid = r1.astype(BF16)
    lo = (r1 - mid.astype(F32)).astype(BF16)
    return hi, mid, lo


def _mm_kernel(*refs, nk, tk, k_valid, a_silu, n_w, epi):
    it = iter(refs)
    a_ref = next(it)
    w_ref = next(it)
    w3_ref = next(it) if n_w == 2 else None
    bias_ref = next(it) if epi == "bias" else None
    x_ref = next(it) if epi == "resid" else None
    gate_ref = next(it) if epi == "resid" else None
    o_ref = next(it)
    acc_ref = next(it) if nk > 1 else None
    acc3_ref = next(it) if (nk > 1 and n_w == 2) else None

    k = pl.program_id(2)
    ragged = (k_valid % tk) != 0
    w_refs = [w_ref, w3_ref][:n_w]

    def dots(masked):
        a = a_ref[...]
        if a_silu:
            a = _silu(a.astype(F32))
        a = a.astype(BF16)
        if masked:
            col = k * tk + lax.broadcasted_iota(I32, a.shape, 1)
            a = jnp.where(col < k_valid, a, jnp.zeros_like(a))
        out = []
        for ref in w_refs:
            w = ref[...]
            if masked:
                row = k * tk + lax.broadcasted_iota(I32, w.shape, 0)
                w = jnp.where(row < k_valid, w, jnp.zeros_like(w))
            out.append(jnp.dot(a, w.astype(BF16), preferred_element_type=F32))
        return out + [None] * (2 - n_w)

    def epilogue(h, h3):
        if epi == "plain":
            o_ref[...] = h.astype(o_ref.dtype)
        elif epi == "bias":
            o_ref[...] = (h + bias_ref[...]).astype(o_ref.dtype)
        elif epi == "swiglu":
            o_ref[...] = (_silu(h) * h3).astype(o_ref.dtype)
        else:
            o_ref[...] = (x_ref[...] + gate_ref[...] * h).astype(o_ref.dtype)

    if nk == 1:
        epilogue(*dots(ragged))
        return

    @pl.when(k == 0)
    def _():
        d, d3 = dots(False)
        acc_ref[...] = d
        if n_w == 2:
            acc3_ref[...] = d3

    if nk > 2:
        @pl.when((k > 0) & (k < nk - 1))
        def _():
            d, d3 = dots(False)
            acc_ref[...] += d
            if n_w == 2:
                acc3_ref[...] += d3

    @pl.when(k == nk - 1)
    def _():
        d, d3 = dots(ragged)
        epilogue(acc_ref[...] + d, acc3_ref[...] + d3 if n_w == 2 else None)


def _mm(a, w, *, layer=0, w3=None, bias=None, resid=None, gate=None, gate_col=0, gate_group_rows=None,
        out_dtype=F32, tm, tn, tk, a_silu=False, name="mm"):
    m, kdim = a.shape
    n = w.shape[-1]
    tm = min(tm, m)
    tn = min(tn, n)
    tk = min(tk, kdim)
    nk = pl.cdiv(kdim, tk)
    grid = (pl.cdiv(m, tm), pl.cdiv(n, tn), nk)
    n_w = 2 if w3 is not None else 1
    epi = "swiglu" if w3 is not None else "bias" if bias is not None else "resid" if resid is not None else "plain"

    a_bufs = 1 if (nk == 1 and grid[1] > 1) else 2
    a_mode = dict(pipeline_mode=pl.Buffered(1)) if a_bufs == 1 else {}
    in_specs = [pl.BlockSpec((tm, tk), lambda i, j, k: (i, k), **a_mode),
                pl.BlockSpec((None, tk, tn), lambda i, j, k: (layer, k, j))]
    args = [a, w]
    if n_w == 2:
        in_specs.append(pl.BlockSpec((None, tk, tn), lambda i, j, k: (layer, k, j)))
        args.append(w3)
    if epi == "bias":
        in_specs.append(pl.BlockSpec((None, 1, tn), lambda i, j, k: (layer, 0, j)))
        args.append(bias)
    if epi == "resid":
        in_specs.append(pl.BlockSpec((tm, tn), lambda i, j, k: (i, j)))
        args.append(resid)
        r = gate.shape[1]
        nj = n // tn
        if r == 1:
            per = gate_group_rows // tm
            in_specs.append(pl.BlockSpec((None, 1, tn), lambda i, j, k: (i // per, 0, gate_col * nj + j)))
        else:
            in_specs.append(pl.BlockSpec((None, tm, tn), lambda i, j, k: (i, 0, gate_col * nj + j)))
        args.append(gate)

    scratch = []
    if nk > 1:
        scratch = [pltpu.VMEM((tm, tn), F32)] * n_w
    out_b = jnp.dtype(out_dtype).itemsize
    est = (a_bufs * tm * tk * a.dtype.itemsize + 2 * n_w * tk * tn * 4 + n_w * tk * tn * 2
           + (n_w + 1) * tm * tn * 4 + 2 * tm * tn * out_b + (3 * tm * tn * 4 if epi == "resid" else 0))
    kern = functools.partial(_mm_kernel, nk=nk, tk=tk, k_valid=kdim, a_silu=a_silu, n_w=n_w, epi=epi)
    return pl.pallas_call(
        kern,
        out_shape=jax.ShapeDtypeStruct((m, n), out_dtype),
        grid=grid,
        in_specs=in_specs,
        out_specs=pl.BlockSpec((tm, tn), lambda i, j, k: (i, j)),
        scratch_shapes=scratch,
        compiler_params=_params(3, est // MIB + 8),
        name=name,
    )(*args)


def _prenorm_kernel(*refs, with_router):
    if with_router:
        x_ref, g_ref, sh_ref, sc_ref, r_ref, o_ref, route_ref = refs
    else:
        x_ref, g_ref, sh_ref, sc_ref, o_ref = refs
    x = x_ref[...]
    y = x * lax.rsqrt(jnp.mean(x * x, axis=-1, keepdims=True) + EPS) * g_ref[...]
    h = y * (1.0 + sc_ref[...]) + sh_ref[...]
    if not with_router:
        o_ref[...] = h.astype(o_ref.dtype)
    else:
        half = h.shape[-1] // 2
        bits = lax.bitcast_convert_type(h.astype(BF16).astype(F32), jnp.uint32)
        o_ref[...] = bits[:, half:] | (bits[:, :half] >> 16)
        h1, h2, _ = _split3(h)
        r = r_ref[...]
        r1, r2, _ = _split3(r)
        logits = (jnp.dot(h1, r1, preferred_element_type=F32)
                  + jnp.dot(h1, r2, preferred_element_type=F32)
                  + jnp.dot(h2, r1, preferred_element_type=F32))
        lane = lax.broadcasted_iota(I32, logits.shape, 1).astype(F32)
        logits = jnp.where(lane < N_EXPERTS, logits, -jnp.inf)
        v1 = jnp.max(logits, axis=-1, keepdims=True)
        i1 = jnp.min(jnp.where(logits == v1, lane, float(LANE)), axis=-1, keepdims=True)
        rest = jnp.where(lane == i1, -jnp.inf, logits)
        v2 = jnp.max(rest, axis=-1, keepdims=True)
        i2 = jnp.min(jnp.where(rest == v2, lane, float(LANE)), axis=-1, keepdims=True)
        e2 = jnp.exp(v2 - v1)
        w1 = 1.0 / (1.0 + e2)
        w2 = e2 / (1.0 + e2)
        route = jnp.where(lane == 0, i1, jnp.where(lane == 1, i2, jnp.where(lane == 2, w1, jnp.where(lane == 3, w2, 0.0))))
        route_ref[...] = route


def _prenorm(x3, norm_g4, g_idx, mod, shift_col, router=None, *, tt):
    b, t, d = x3.shape
    tt = min(tt, t)
    r = mod.shape[1]
    if r == 1:
        mod_spec = lambda c: pl.BlockSpec((None, 1, d), lambda bi, ti: (bi, 0, c))
    else:
        mod_spec = lambda c: pl.BlockSpec((None, tt, d), lambda bi, ti: (bi, ti, c))
    in_specs = [pl.BlockSpec((None, tt, d), lambda bi, ti: (bi, ti, 0)),
                pl.BlockSpec((None, 1, d), lambda bi, ti: (g_idx, 0, 0)),
                mod_spec(shift_col), mod_spec(shift_col + 1)]
    args = [x3, norm_g4, mod, mod]
    if router is None:
        out_shape = [jax.ShapeDtypeStruct((b, t, d), BF16)]
        out_specs = [pl.BlockSpec((None, tt, d), lambda bi, ti: (bi, ti, 0))]
    else:
        out_shape = [jax.ShapeDtypeStruct((b, t, d // 2), jnp.uint32)]
        out_specs = [pl.BlockSpec((None, tt, d // 2), lambda bi, ti: (bi, ti, 0))]
    if router is not None:
        in_specs.append(pl.BlockSpec((d, LANE), lambda bi, ti: (0, 0)))
        args.append(router)
        out_shape.append(jax.ShapeDtypeStruct((b, t, LANE), F32))
        out_specs.append(pl.BlockSpec((None, tt, LANE), lambda bi, ti: (bi, ti, 0)))
    res = pl.pallas_call(
        functools.partial(_prenorm_kernel, with_router=router is not None),
        out_shape=out_shape, grid=(b, t // tt), in_specs=in_specs, out_specs=out_specs,
        compiler_params=_params(2, 48), name="prenorm_router" if router is not None else "prenorm",
    )(*args)
    return res if router is not None else res[0]


def _mla_post_kernel(p_ref, qn_ref, kn_ref, cos_ref, sin_ref, cq_ref, ckv_ref, kpe_ref):
    cq = p_ref[:, :Q_LORA]
    cq = cq * lax.rsqrt(jnp.mean(cq * cq, axis=-1, keepdims=True) + EPS) * qn_ref[...]
    cq_ref[...] = cq.astype(cq_ref.dtype)
    c = p_ref[:, Q_LORA:Q_LORA + KV_LORA]
    ckv_ref[...] = c * lax.rsqrt(jnp.mean(c * c, axis=-1, keepdims=True) + EPS) * kn_ref[...]
    slab = p_ref[:, Q_LORA + KV_LORA:]
    rot = pltpu.roll(slab, ROPE_DIM, 1)
    kpe = slab * cos_ref[...] + rot * sin_ref[...]
    kpe_ref[...] = kpe[:, :ROPE_DIM]


def _mla_post(proj3, q_norm, kv_norm, cos_p, sin_p, *, tt):
    b, t, n = proj3.shape
    tt = min(tt, t)
    return pl.pallas_call(
        _mla_post_kernel,
        out_shape=[jax.ShapeDtypeStruct((b, t, Q_LORA), BF16),
                   jax.ShapeDtypeStruct((b, t, KV_LORA), F32),
                   jax.ShapeDtypeStruct((b, t, ROPE_DIM), F32)],
        grid=(b, t // tt),
        in_specs=[pl.BlockSpec((None, tt, n), lambda bi, ti: (bi, ti, 0)),
                  pl.BlockSpec((1, Q_LORA), lambda bi, ti: (0, 0)),
                  pl.BlockSpec((1, KV_LORA), lambda bi, ti: (0, 0)),
                  pl.BlockSpec((tt, LANE), lambda bi, ti: (ti, 0)),
                  pl.BlockSpec((tt, LANE), lambda bi, ti: (ti, 0))],
        out_specs=[pl.BlockSpec((None, tt, Q_LORA), lambda bi, ti: (bi, ti, 0)),
                   pl.BlockSpec((None, tt, KV_LORA), lambda bi, ti: (bi, ti, 0)),
                   pl.BlockSpec((None, tt, ROPE_DIM), lambda bi, ti: (bi, ti, 0))],
        compiler_params=_params(2, 32), name="mla_post",
    )(proj3, q_norm, kv_norm, cos_p, sin_p)


def _attn_prompt_kernel(q_ref, cos_ref, sin_ref, kt_ref, v_ref, wuk_ref, wuv_ref, o_ref,
                        qs_ref, m_ref, l_ref, acc_ref, *, tq, tkv):
    qi = pl.program_id(1)
    hn = MLA_HEADS * NOPE_DIM
    hr = MLA_HEADS * ROPE_DIM
    n_pair = MLA_HEADS // 2
    cos_t = jnp.concatenate([cos_ref[...]] * (hr // LANE), axis=1)
    sin_t = jnp.concatenate([sin_ref[...]] * (hr // LANE), axis=1)
    qpe = (q_ref[:, hn:hn + hr].astype(F32) * cos_t
           + q_ref[:, hn + hr:hn + 2 * hr].astype(F32) * sin_t).astype(BF16)
    n_full = (qi * tq) // tkv
    rows = n_pair * tq

    for par in range(2):
        for mi in range(n_pair):
            h = 2 * mi + par
            ql = jnp.dot(q_ref[:, h * NOPE_DIM:(h + 1) * NOPE_DIM], wuk_ref[h], preferred_element_type=F32)
            qs_ref[mi * tq:(mi + 1) * tq, :KV_LORA] = ql.astype(BF16)
            qs_ref[mi * tq:(mi + 1) * tq, KV_LORA:] = qpe[:, mi * LANE:(mi + 1) * LANE]
        m_ref[...] = jnp.full(m_ref.shape, -jnp.inf, F32)
        l_ref[...] = jnp.zeros(l_ref.shape, F32)
        acc_ref[...] = jnp.zeros(acc_ref.shape, F32)

        def step(j, masked):
            s = jnp.dot(qs_ref[...], kt_ref[par, j], preferred_element_type=F32) * ATTN_SCALE
            if masked:
                r = lax.broadcasted_iota(I32, (rows, tkv), 0)
                q_pos = qi * tq + (r & (tq - 1))
                k_pos = j * tkv + lax.broadcasted_iota(I32, (rows, tkv), 1)
                s = jnp.where(k_pos <= q_pos, s, NEG_INF)
            m_prev = m_ref[...]
            m_new = jnp.maximum(m_prev, jnp.max(s, axis=1, keepdims=True))
            alpha = jnp.exp(m_prev - m_new)
            p = jnp.exp(s - jnp.tile(m_new, (1, tkv // LANE)))
            l_ref[...] = alpha * l_ref[...] + jnp.sum(p, axis=1, keepdims=True)
            pv = jnp.dot(p.astype(BF16), v_ref[j], preferred_element_type=F32)
            acc_ref[...] = acc_ref[...] * jnp.tile(alpha, (1, KV_LORA // LANE)) + pv
            m_ref[...] = m_new

        def body(j, carry):
            step(j, False)
            return carry

        lax.fori_loop(0, n_full, body, 0)
        step(n_full, True)

        inv = 1.0 / l_ref[...]
        for mi in range(n_pair):
            h = 2 * mi + par
            sl = slice(mi * tq, (mi + 1) * tq)
            lat = (acc_ref[sl, :] * jnp.tile(inv[sl, :], (1, KV_LORA // LANE))).astype(BF16)
            o_ref[:, h * V_DIM:(h + 1) * V_DIM] = jnp.dot(
                lat, wuv_ref[h], preferred_element_type=F32).astype(o_ref.dtype)


def _attn_prompt(q3, cos128, sin128, kt, v4, wuk, wuv, *, tq, tkv):
    b, t, nq = q3.shape
    nkv = t // tkv
    kc = KV_LORA + LANE
    rows = (MLA_HEADS // 2) * tq
    one = pl.Buffered(1)
    return pl.pallas_call(
        functools.partial(_attn_prompt_kernel, tq=tq, tkv=tkv),
        out_shape=jax.ShapeDtypeStruct((b, t, MLA_HEADS * V_DIM), BF16),
        grid=(b, t // tq),
        in_specs=[pl.BlockSpec((None, tq, nq), lambda bi, qi: (bi, qi, 0)),
                  pl.BlockSpec((tq, LANE), lambda bi, qi: (qi, 0)),
                  pl.BlockSpec((tq, LANE), lambda bi, qi: (qi, 0)),
                  pl.BlockSpec((None, 2, nkv, kc, tkv), lambda bi, qi: (bi, 0, 0, 0, 0), pipeline_mode=one),
                  pl.BlockSpec((None, nkv, tkv, KV_LORA), lambda bi, qi: (bi, 0, 0, 0), pipeline_mode=one),
                  pl.BlockSpec((MLA_HEADS, NOPE_DIM, KV_LORA), lambda bi, qi: (0, 0, 0), pipeline_mode=one),
                  pl.BlockSpec((MLA_HEADS, KV_LORA, V_DIM), lambda bi, qi: (0, 0, 0), pipeline_mode=one)],
        out_specs=pl.BlockSpec((None, tq, MLA_HEADS * V_DIM), lambda bi, qi: (bi, qi, 0)),
        scratch_shapes=[pltpu.VMEM((rows, kc), BF16), pltpu.VMEM((rows, LANE), F32),
                        pltpu.VMEM((rows, LANE), F32), pltpu.VMEM((rows, KV_LORA), F32)],
        compiler_params=_params(2, 56), name="mla_attn_prompt",
    )(q3, cos128, sin128, kt, v4, wuk, wuv)


def _head_mm_kernel(a_ref, w_ref, o_ref):
    o_ref[...] = jnp.dot(a_ref[...].astype(BF16), w_ref[...], preferred_element_type=F32).astype(o_ref.dtype)


def _head_mm(a, w, ka, na):
    m = a.shape[0]
    h = w.shape[0]
    return pl.pallas_call(
        _head_mm_kernel,
        out_shape=jax.ShapeDtypeStruct((m, h * na), BF16),
        grid=(h,),
        in_specs=[pl.BlockSpec((m, ka), lambda hi: (0, hi)),
                  pl.BlockSpec((None, ka, na), lambda hi: (hi, 0, 0))],
        out_specs=pl.BlockSpec((m, na), lambda hi: (0, hi)),
        compiler_params=_params(1, 16), name="head_mm",
    )(a, w)


def _rope_q_kernel(x_ref, xr_ref, cos_ref, sin_ref, o_ref):
    o_ref[...] = (x_ref[...].astype(F32) * cos_ref[...] + xr_ref[...].astype(F32) * sin_ref[...]).astype(o_ref.dtype)


def _rope_q(q, cos_t, sin_t):
    m = q.shape[0]
    hr = MLA_HEADS * ROPE_DIM
    nope_blocks = (MLA_HEADS * NOPE_DIM) // hr
    return pl.pallas_call(
        _rope_q_kernel,
        out_shape=jax.ShapeDtypeStruct((m, hr), BF16),
        grid=(1,),
        in_specs=[pl.BlockSpec((m, hr), lambda i: (0, nope_blocks)),
                  pl.BlockSpec((m, hr), lambda i: (0, nope_blocks + 1)),
                  pl.BlockSpec((m, hr), lambda i: (0, 0)),
                  pl.BlockSpec((m, hr), lambda i: (0, 0))],
        out_specs=pl.BlockSpec((m, hr), lambda i: (0, 0)),
        compiler_params=_params(1, 16), name="rope_q",
    )(q, q, cos_t, sin_t)


def _attn_decode_kernel(pt_ref, ql_ref, qp_ref, cn_ref, kn_ref, *rest, ns, pps, n_steps):
    n_pg = ns * pps
    ckv_refs = rest[:n_pg]
    kpe_refs = rest[n_pg:2 * n_pg]
    o_ref = rest[2 * n_pg]
    m_ref, l_ref, acc_ref, kbuf_ref, pbuf_ref = rest[2 * n_pg + 1:]
    c = pl.program_id(1)
    nt = (((1,), (1,)), ((), ()))

    for si in range(ns):
        for p in range(pps):
            kbuf_ref[si, p * PAGE_SIZE:(p + 1) * PAGE_SIZE, :] = ckv_refs[si * pps + p][...].astype(BF16)
            pbuf_ref[si, :, p * PAGE_SIZE:(p + 1) * PAGE_SIZE] = kpe_refs[si * pps + p][...].astype(BF16)

    @pl.when(c == 0)
    def _():
        for si in range(ns):
            cn = cn_ref[si]
            kn = kn_ref[si]
            s_new = (jnp.sum(ql_ref[si].astype(F32) * cn, axis=-1, keepdims=True)
                     + jnp.sum(qp_ref[si].astype(F32) * kn, axis=-1, keepdims=True)) * ATTN_SCALE
            m_ref[si] = jnp.broadcast_to(s_new, m_ref.shape[1:])
            l_ref[si] = jnp.ones(l_ref.shape[1:], F32)
            acc_ref[si] = jnp.broadcast_to(cn, acc_ref.shape[1:])

    for si in range(ns):
        kb = kbuf_ref[si]
        s = (lax.dot_general(ql_ref[si], kb, nt, preferred_element_type=F32)
             + jnp.dot(qp_ref[si], pbuf_ref[si], preferred_element_type=F32)) * ATTN_SCALE
        m_prev = m_ref[si]
        m_new = jnp.maximum(m_prev, jnp.max(s, axis=1, keepdims=True))
        alpha = jnp.exp(m_prev - m_new)
        p_all = jnp.exp(s - jnp.tile(m_new, (1, pps))).astype(BF16)
        l_ref[si] = alpha * l_ref[si] + jnp.sum(p_all.astype(F32), axis=1, keepdims=True)
        pv = jnp.dot(p_all, kb, preferred_element_type=F32)
        acc_ref[si] = acc_ref[si] * jnp.tile(alpha, (1, KV_LORA // LANE)) + pv
        m_ref[si] = m_new

    @pl.when(c == n_steps - 1)
    def _():
        for si in range(ns):
            o_ref[si] = (acc_ref[si] / jnp.tile(l_ref[si], (1, KV_LORA // LANE))).astype(o_ref.dtype)


def _attn_decode(page_table, ql3, qp3, ckv_new3, kpe_new3, cache_ckv, cache_kpe_t, layer, *, ns, pps):
    nb, n_pages = page_table.shape
    n_steps = n_pages // pps
    h = MLA_HEADS

    def page_spec(rows, width, si, p):
        return pl.BlockSpec((None, None, rows, width),
                            lambda bi, ci, pt: (layer, pt[bi * ns + si, ci * pps + p], 0, 0))

    in_specs = [pl.BlockSpec((ns, h, KV_LORA), lambda bi, ci, pt: (bi, 0, 0)),
                pl.BlockSpec((ns, h, ROPE_DIM), lambda bi, ci, pt: (bi, 0, 0)),
                pl.BlockSpec((ns, 1, KV_LORA), lambda bi, ci, pt: (bi, 0, 0)),
                pl.BlockSpec((ns, 1, ROPE_DIM), lambda bi, ci, pt: (bi, 0, 0))]
    in_specs += [page_spec(PAGE_SIZE, KV_LORA, si, p) for si in range(ns) for p in range(pps)]
    in_specs += [page_spec(ROPE_DIM, PAGE_SIZE, si, p) for si in range(ns) for p in range(pps)]
    grid_spec = pltpu.PrefetchScalarGridSpec(
        num_scalar_prefetch=1, grid=(nb // ns, n_steps), in_specs=in_specs,
        out_specs=pl.BlockSpec((ns, h, KV_LORA), lambda bi, ci, pt: (bi, 0, 0)),
        scratch_shapes=[pltpu.VMEM((ns, h, LANE), F32), pltpu.VMEM((ns, h, LANE), F32),
                        pltpu.VMEM((ns, h, KV_LORA), F32),
                        pltpu.VMEM((ns, pps * PAGE_SIZE, KV_LORA), BF16),
                        pltpu.VMEM((ns, ROPE_DIM, pps * PAGE_SIZE), BF16)])
    n_pg = ns * pps
    return pl.pallas_call(
        functools.partial(_attn_decode_kernel, ns=ns, pps=pps, n_steps=n_steps),
        out_shape=jax.ShapeDtypeStruct((nb, h, KV_LORA), BF16),
        grid_spec=grid_spec,
        compiler_params=_params(2, 40), name="mla_attn_decode",
    )(page_table, ql3, qp3, ckv_new3, kpe_new3, *([cache_ckv] * n_pg), *([cache_kpe_t] * n_pg))


def _hgrn_prompt_kernel(q_ref, f_ref, i_ref, lb_ref, o_ref, s_out_ref, s_ref, *, tt, hb, n_tb):
    tb = pl.program_id(2)

    @pl.when(tb == 0)
    def _():
        s_ref[...] = jnp.zeros(s_ref.shape, F32)

    c = HG_CHUNK
    dk = HG_KDIM
    lb = lb_ref[...]
    sg = jax.nn.sigmoid(f_ref[...])
    log_f = jnp.log(lb + (1.0 - lb) * sg)
    kk = (1.0 - lb) * (1.0 - sg)
    qs = _silu(q_ref[...])
    vv = i_ref[...]
    row = lax.broadcasted_iota(I32, (c, c), 0)
    col = lax.broadcasted_iota(I32, (c, c), 1)
    causal = col <= row
    tril = causal.astype(BF16)
    mid = (c - 1) // 2
    nt = (((1,), (1,)), ((), ()))

    for ci in range(tt // c):
        rs = slice(ci * c, (ci + 1) * c)
        lf1, lf2, lf3 = _split3(log_f[rs])
        cum = (jnp.dot(tril, lf1, preferred_element_type=F32)
               + jnp.dot(tril, lf2, preferred_element_type=F32)
               + jnp.dot(tril, lf3, preferred_element_type=F32))
        ref = cum[mid:mid + 1]
        last = cum[c - 1:c]
        qc = qs[rs]
        kc = kk[rs]
        vb = vv[rs].astype(BF16)
        qd = (qc * jnp.exp(cum - ref)).astype(BF16)
        kd = (kc * jnp.exp(ref - cum)).astype(BF16)
        q0 = (qc * jnp.exp(cum)).astype(BF16)
        k2 = kc * jnp.exp(last - cum)
        dec = jnp.exp(last)
        for h in range(hb):
            ls = slice(h * dk, (h + 1) * dk)
            a = lax.dot_general(qd[:, ls], kd[:, ls], nt, preferred_element_type=F32)
            a = jnp.where(causal, a, 0.0).astype(BF16)
            s_prev = s_ref[h]
            o = (jnp.dot(q0[:, ls], s_prev.astype(BF16), preferred_element_type=F32)
                 + jnp.dot(a, vb[:, ls], preferred_element_type=F32))
            o_ref[rs, ls] = o
            dec_col = jnp.broadcast_to(dec[:, ls], (dk, dk)).T
            k2t = k2[:, ls].T.astype(BF16)
            s_ref[h] = dec_col * s_prev + jnp.dot(k2t, vb[:, ls], preferred_element_type=F32)

    @pl.when(tb == n_tb - 1)
    def _():
        s_out_ref[...] = s_ref[...]


def _hgrn_prompt(proj3, lb2, *, tt, hb):
    b, t, n4 = proj3.shape
    d = n4 // 4
    heads = d // HG_KDIM
    w = hb * HG_KDIM
    nhg = heads // hb
    n_tb = t // tt
    return pl.pallas_call(
        functools.partial(_hgrn_prompt_kernel, tt=tt, hb=hb, n_tb=n_tb),
        out_shape=[jax.ShapeDtypeStruct((b, t, d), F32),
                   jax.ShapeDtypeStruct((b, heads, HG_KDIM, HG_KDIM), F32)],
        grid=(b, nhg, n_tb),
        in_specs=[pl.BlockSpec((None, tt, w), lambda bi, hi, ti: (bi, ti, hi)),
                  pl.BlockSpec((None, tt, w), lambda bi, hi, ti: (bi, ti, nhg + hi)),
                  pl.BlockSpec((None, tt, w), lambda bi, hi, ti: (bi, ti, 2 * nhg + hi)),
                  pl.BlockSpec((1, w), lambda bi, hi, ti: (0, hi))],
        out_specs=[pl.BlockSpec((None, tt, w), lambda bi, hi, ti: (bi, ti, hi)),
                   pl.BlockSpec((None, hb, HG_KDIM, HG_KDIM), lambda bi, hi, ti: (bi, hi, 0, 0))],
        scratch_shapes=[pltpu.VMEM((hb, HG_KDIM, HG_KDIM), F32)],
        compiler_params=_params(3, 32), name="hgrn_prompt",
    )(proj3, proj3, proj3, lb2)


def _hgrn_decode_kernel(p_ref, lb_ref, s_ref, o_ref, s_out_ref):
    heads = s_ref.shape[0]
    dk = HG_KDIM
    lb = lb_ref[...]
    sg = jax.nn.sigmoid(p_ref[1])
    fg = lb + (1.0 - lb) * sg
    kk = (1.0 - lb) * (1.0 - sg)
    qs = _silu(p_ref[0])
    vv = p_ref[2]
    outs = []
    for h in range(heads):
        f_col = jnp.broadcast_to(fg[h:h + 1], (dk, dk)).T
        k_col = jnp.broadcast_to(kk[h:h + 1], (dk, dk)).T
        q_col = jnp.broadcast_to(qs[h:h + 1], (dk, dk)).T
        s_new = f_col * s_ref[h] + k_col * vv[h:h + 1]
        s_out_ref[h] = s_new
        outs.append(jnp.sum(q_col * s_new, axis=0, keepdims=True))
    o_ref[...] = jnp.concatenate(outs, axis=0)


def _hgrn_decode(proj4, lb2, state):
    b, _, heads, dk = proj4.shape
    return pl.pallas_call(
        _hgrn_decode_kernel,
        out_shape=[jax.ShapeDtypeStruct((b, heads, dk), F32),
                   jax.ShapeDtypeStruct(state.shape, F32)],
        grid=(b,),
        in_specs=[pl.BlockSpec((None, 4, heads, dk), lambda bi: (bi, 0, 0, 0)),
                  pl.BlockSpec((heads, dk), lambda bi: (0, 0)),
                  pl.BlockSpec((None, heads, dk, dk), lambda bi: (bi, 0, 0, 0))],
        out_specs=[pl.BlockSpec((None, heads, dk), lambda bi: (bi, 0, 0)),
                   pl.BlockSpec((None, heads, dk, dk), lambda bi: (bi, 0, 0, 0))],
        compiler_params=_params(1, 32), name="hgrn_decode",
    )(proj4, lb2, state)


def _gated_norm_kernel(o_ref, g_ref, gn_ref, out_ref):
    o = o_ref[...]
    y = o * lax.rsqrt(jnp.mean(o * o, axis=-1, keepdims=True) + EPS) * gn_ref[...]
    out_ref[...] = (y * _silu(g_ref[...])).astype(out_ref.dtype)


def _gated_norm(o3, proj3, g_norm, *, tt):
    b, t, d = o3.shape
    tt = min(tt, t)
    return pl.pallas_call(
        _gated_norm_kernel,
        out_shape=jax.ShapeDtypeStruct((b, t, d), BF16),
        grid=(b, t // tt),
        in_specs=[pl.BlockSpec((None, tt, d), lambda bi, ti: (bi, ti, 0)),
                  pl.BlockSpec((None, tt, d), lambda bi, ti: (bi, ti, 3)),
                  pl.BlockSpec((1, d), lambda bi, ti: (0, 0))],
        out_specs=pl.BlockSpec((None, tt, d), lambda bi, ti: (bi, ti, 0)),
        compiler_params=_params(2, 48), name="gated_norm",
    )(o3, proj3, g_norm)


def _moe_up_kernel(hl_ref, he_ref, nv_ref, a_ref, w1_ref, w3_ref, o_ref):
    i = pl.program_id(1)

    @pl.when(i < nv_ref[0])
    def _():
        words = a_ref[...]
        half = words.shape[1]
        a_lo = lax.bitcast_convert_type(words << 16, F32).astype(BF16)
        a_hi = lax.bitcast_convert_type(words & jnp.uint32(0xFFFF0000), F32).astype(BF16)

        def proj(w_ref):
            w = w_ref[...].astype(BF16)
            return (jnp.dot(a_lo, w[:half], preferred_element_type=F32)
                    + jnp.dot(a_hi, w[half:], preferred_element_type=F32))

        o_ref[...] = (_silu(proj(w1_ref)) * proj(w3_ref)).astype(o_ref.dtype)


def _moe_up(half_list, half_expert, n_occupied, xs, w1, w3, *, tm, tn, max_occupied):
    r, dh = xs.shape
    d = 2 * dh
    n = w1.shape[-1]
    tn = min(tn, n)
    grid_spec = pltpu.PrefetchScalarGridSpec(
        num_scalar_prefetch=3, grid=(n // tn, min(r // tm, max_occupied)),
        in_specs=[pl.BlockSpec((tm, dh), lambda j, i, hl, he, nv: (hl[i], 0)),
                  pl.BlockSpec((None, d, tn), lambda j, i, hl, he, nv: (he[i], 0, j)),
                  pl.BlockSpec((None, d, tn), lambda j, i, hl, he, nv: (he[i], 0, j))],
        out_specs=pl.BlockSpec((tm, tn), lambda j, i, hl, he, nv: (hl[i], j)))
    est = 2 * tm * dh * 4 + 2 * tm * d * 2 + 4 * d * tn * 4 + 2 * d * tn * 2 + 2 * tm * tn * 2 + 3 * tm * tn * 4
    return pl.pallas_call(
        _moe_up_kernel,
        out_shape=jax.ShapeDtypeStruct((r, n), BF16),
        grid_spec=grid_spec,
        compiler_params=_params(2, est // MIB + 4), name="moe_up",
    )(half_list, half_expert, n_occupied, xs, w1, w3)


def _moe_down_kernel(tl_ref, te_ref, sv_ref, nv_ref, a_ref, w_ref, o_ref, *, half):
    i = pl.program_id(0)
    k = pl.program_id(2)
    valid = i < nv_ref[0]
    both = valid & (sv_ref[i] != 0)
    first_only = valid & (sv_ref[i] == 0)

    @pl.when(both & (k == 0))
    def _():
        o_ref[...] = jnp.dot(a_ref[...], w_ref[...].astype(BF16), preferred_element_type=F32)

    @pl.when(both & (k > 0))
    def _():
        o_ref[...] += jnp.dot(a_ref[...], w_ref[...].astype(BF16), preferred_element_type=F32)

    @pl.when(first_only & (k == 0))
    def _():
        o_ref[:half, :] = jnp.dot(a_ref[:half, :], w_ref[...].astype(BF16), preferred_element_type=F32)
        o_ref[half:, :] = jnp.zeros((o_ref.shape[0] - half, o_ref.shape[1]), o_ref.dtype)

    @pl.when(first_only & (k > 0))
    def _():
        o_ref[:half, :] += jnp.dot(a_ref[:half, :], w_ref[...].astype(BF16), preferred_element_type=F32)


def _moe_down(tile_list, tile_expert, second_valid, n_occupied, act, w2, *, tm, tn, tk):
    r, kdim = act.shape
    n = w2.shape[-1]
    tn = min(tn, n)
    tk = min(tk, kdim)
    nk = kdim // tk
    nn = n // tn

    def kk(i, k, nv):
        return jnp.where(i < nv[0], k, nk - 1)

    def jj(i, j, nv):
        return jnp.where(i < nv[0], j, nn - 1)

    grid_spec = pltpu.PrefetchScalarGridSpec(
        num_scalar_prefetch=4, grid=(r // tm, nn, nk),
        in_specs=[pl.BlockSpec((tm, tk), lambda i, j, k, tl, te, sv, nv: (tl[i], kk(i, k, nv))),
                  pl.BlockSpec((None, tk, tn), lambda i, j, k, tl, te, sv, nv: (te[i], kk(i, k, nv), jj(i, j, nv)))],
        out_specs=pl.BlockSpec((tm, tn), lambda i, j, k, tl, te, sv, nv: (tl[i], jj(i, j, nv))))
    est = 2 * tm * tk * 2 + 2 * tk * tn * 4 + tk * tn * 2 + 2 * tm * tn * 4 + tm * tn * 4
    return pl.pallas_call(
        functools.partial(_moe_down_kernel, half=tm // 2),
        out_shape=jax.ShapeDtypeStruct((r, n), F32),
        grid_spec=grid_spec,
        compiler_params=_params(3, est // MIB + 6), name="moe_down",
    )(tile_list, tile_expert, second_valid, n_occupied, act, w2)


def _moe_combine_kernel(x_ref, gate_ref, y_ref, w_ref, fn_ref, o_ref):
    w = w_ref[...]
    y = w[:, 0:1] * y_ref[0] + w[:, 1:2] * y_ref[1]
    x = x_ref[...] + gate_ref[...] * y
    o_ref[...] = x * lax.rsqrt(jnp.mean(x * x, axis=-1, keepdims=True) + EPS) * fn_ref[...]


def _moe_combine_final(x3, mod, gate_col, y4, w3, final_norm, *, tt):
    b, t, d = x3.shape
    tt = min(tt, t)
    r = mod.shape[1]
    if r == 1:
        gate_spec = pl.BlockSpec((None, 1, d), lambda bi, ti: (bi, 0, gate_col))
    else:
        gate_spec = pl.BlockSpec((None, tt, d), lambda bi, ti: (bi, ti, gate_col))
    return pl.pallas_call(
        _moe_combine_kernel,
        out_shape=jax.ShapeDtypeStruct((b, t, d), F32),
        grid=(b, t // tt),
        in_specs=[pl.BlockSpec((None, tt, d), lambda bi, ti: (bi, ti, 0)),
                  gate_spec,
                  pl.BlockSpec((TOP_K, None, tt, d), lambda bi, ti: (0, bi, ti, 0)),
                  pl.BlockSpec((None, tt, LANE), lambda bi, ti: (bi, ti, 0)),
                  pl.BlockSpec((1, d), lambda bi, ti: (0, 0))],
        out_specs=pl.BlockSpec((None, tt, d), lambda bi, ti: (bi, ti, 0)),
        compiler_params=_params(2, 48), name="moe_combine_final",
    )(x3, mod, y4, w3, final_norm)


MOE_HALF = 512


def kernel(x_prompt, x_sample, cache_ckv, cache_kpe, state_hgrn, page_table, c_prompt, c_sample,
           ada_w, ada_b, norm_g, mla_w_in, mla_q_norm, mla_w_uq, mla_kv_norm, mla_w_uk, mla_w_uv,
           mla_w_o, hg_w_in, hg_lower_bounds, hg_g_norm, hg_w_o, ffn_w1, ffn_w3, ffn_w2,
           moe_router, moe_w1, moe_w3, moe_w2, final_norm):
    bp, seq, d = x_prompt.shape
    bs, dec_seq, _ = x_sample.shape
    depth = ada_w.shape[0]
    assert depth == 2 and dec_seq == 1
    mp = bp * seq
    past_len = page_table.shape[1] * PAGE_SIZE
    heads = MLA_HEADS
    hg_heads = d // HG_KDIM

    n_c = bp + bs
    pad_c = (-n_c) % 8
    c_all = jnp.concatenate([c_prompt, c_sample, jnp.zeros((pad_c, d), F32)], axis=0)
    ada_b3 = ada_b.reshape(depth, 1, N_ADA * d)
    mods_p, mods_s = [], []
    for i in range(depth):
        mod = _mm(c_all, ada_w, layer=i, bias=ada_b3, out_dtype=F32, tm=n_c + pad_c, tn=1024, tk=2048,
                  a_silu=True, name="ada_mod")
        mods_p.append(mod[:bp].reshape(bp, 1, N_ADA * d))
        mods_s.append(mod[bp:bp + bs].reshape(1, bs, N_ADA * d))

    norm_g4 = norm_g.reshape(depth * 2, 1, d)
    xp = x_prompt
    xs = x_sample.reshape(1, bs, d)

    inv = ROPE_THETA ** (-jnp.arange(0, ROPE_DIM, 2, dtype=F32) / ROPE_DIM)

    def tables(pos):
        ang = pos.astype(F32)[:, None] * inv[None, :]
        cos2 = jnp.concatenate([jnp.cos(ang)] * 2, axis=1)
        sin2 = jnp.concatenate([jnp.sin(ang)] * 2, axis=1)
        return cos2, sin2

    cos_p, sin_p = tables(jnp.arange(seq))
    cos_s, sin_s = tables(jnp.broadcast_to(past_len + jnp.arange(dec_seq), (bs,)))
    zpad = lambda a: jnp.concatenate([a, jnp.zeros_like(a)], axis=1)
    dup = lambda a: jnp.concatenate([a, a], axis=1)

    w_in = mla_w_in[0]
    kpe_w = w_in[:, Q_LORA + KV_LORA:]
    half = ROPE_DIM // 2
    w_in_aug = jnp.concatenate([w_in, -kpe_w[:, half:], kpe_w[:, :half]], axis=1)[None]
    wq = mla_w_uq[0].reshape(Q_LORA, heads, NOPE_DIM + ROPE_DIM)
    wq_rope = wq[:, :, NOPE_DIM:]
    wq_rot = jnp.concatenate([-wq_rope[..., half:], wq_rope[..., :half]], axis=-1)
    w_uq_aug = jnp.concatenate([wq[:, :, :NOPE_DIM].reshape(Q_LORA, -1), wq_rope.reshape(Q_LORA, -1),
                                wq_rot.reshape(Q_LORA, -1)], axis=1)[None]
    wuk = jnp.transpose(mla_w_uk[0], (1, 2, 0)).astype(BF16)
    wuv = jnp.transpose(mla_w_uv[0], (1, 0, 2)).astype(BF16)
    q_norm = mla_q_norm[0].reshape(1, Q_LORA)
    kv_norm = mla_kv_norm[0].reshape(1, KV_LORA)

    hp = _prenorm(xp, norm_g4, 0, mods_p[0], 0, tt=256)
    proj_p = _mm(hp.reshape(mp, d), w_in_aug, out_dtype=F32, tm=1024, tn=2048, tk=1024, name="mla_in")
    cq_p, ckv_p, kpe_p = _mla_post(proj_p.reshape(bp, seq, -1), q_norm, kv_norm, zpad(cos_p), zpad(sin_p), tt=512)
    q_p = _mm(cq_p.reshape(mp, Q_LORA), w_uq_aug, out_dtype=BF16, tm=2048, tn=1024, tk=1024, name="mla_uq")
    tkv = 512
    nkv = seq // tkv
    ckv_b = ckv_p.astype(BF16)
    kpe_b = kpe_p.astype(BF16)
    zk = jnp.zeros_like(kpe_b)
    kcat = jnp.stack([jnp.concatenate([ckv_b, kpe_b, zk], axis=-1),
                      jnp.concatenate([ckv_b, zk, kpe_b], axis=-1)], axis=1)
    kt = jnp.swapaxes(kcat.reshape(bp, 2, nkv, tkv, KV_LORA + LANE), -1, -2)
    v4 = ckv_b.reshape(bp, nkv, tkv, KV_LORA)
    o_p = _attn_prompt(q_p.reshape(bp, seq, -1), dup(cos_p), dup(sin_p), kt, v4, wuk, wuv, tq=128, tkv=tkv)
    xp = _mm(o_p.reshape(mp, d), mla_w_o, resid=xp.reshape(mp, d), gate=mods_p[0], gate_col=2,
             gate_group_rows=seq, out_dtype=F32, tm=1024, tn=512, tk=d, name="mla_o").reshape(bp, seq, d)

    hs = _prenorm(xs, norm_g4, 0, mods_s[0], 0, tt=bs)
    proj_s = _mm(hs.reshape(bs, d), w_in_aug, out_dtype=F32, tm=bs, tn=2048, tk=2048, name="mla_in_s")
    cq_s, ckv_s, kpe_s = _mla_post(proj_s.reshape(1, bs, -1), q_norm, kv_norm, zpad(cos_s), zpad(sin_s), tt=bs)
    q_s = _mm(cq_s.reshape(bs, Q_LORA), w_uq_aug, out_dtype=BF16, tm=bs, tn=2048, tk=1024, name="mla_uq_s")
    ql_s = _head_mm(q_s, wuk, NOPE_DIM, KV_LORA).reshape(bs, heads, KV_LORA)
    qp_s = _rope_q(q_s, jnp.tile(cos_s, (1, heads)), jnp.tile(sin_s, (1, heads))).reshape(bs, heads, ROPE_DIM)
    lat_s = _attn_decode(page_table, ql_s, qp_s, ckv_s.reshape(bs, 1, KV_LORA), kpe_s.reshape(bs, 1, ROPE_DIM),
                         cache_ckv, jnp.swapaxes(cache_kpe, 2, 3), 0, ns=2, pps=16)
    o_s = _head_mm(lat_s.reshape(bs, heads * KV_LORA), wuv, KV_LORA, V_DIM)
    xs = _mm(o_s, mla_w_o, resid=xs.reshape(bs, d), gate=mods_s[0], gate_col=2, out_dtype=F32,
             tm=bs, tn=1024, tk=2048, name="mla_o_s").reshape(1, bs, d)

    hp = _prenorm(xp, norm_g4, 1, mods_p[0], 3, tt=256)
    act_p = _mm(hp.reshape(mp, d), ffn_w1, w3=ffn_w3, out_dtype=BF16, tm=2048, tn=256, tk=d, name="ffn_up")
    xp = _mm(act_p, ffn_w2, resid=xp.reshape(mp, d), gate=mods_p[0], gate_col=5, gate_group_rows=seq,
             out_dtype=F32, tm=1024, tn=1024, tk=1024, name="ffn_down").reshape(bp, seq, d)
    hs = _prenorm(xs, norm_g4, 1, mods_s[0], 3, tt=bs)
    act_s = _mm(hs.reshape(bs, d), ffn_w1, w3=ffn_w3, out_dtype=BF16, tm=bs, tn=1024, tk=1024, name="ffn_up_s")
    xs = _mm(act_s, ffn_w2, resid=xs.reshape(bs, d), gate=mods_s[0], gate_col=5, out_dtype=F32,
             tm=bs, tn=1024, tk=2048, name="ffn_down_s").reshape(1, bs, d)

    p_lb = jax.nn.softmax(hg_lower_bounds.astype(F32), axis=0)
    lb = (jnp.cumsum(p_lb, axis=0) - p_lb[0:1])[1].reshape(1, d)

    hp = _prenorm(xp, norm_g4, 2, mods_p[1], 0, tt=256)
    proj_p = _mm(hp.reshape(mp, d), hg_w_in, out_dtype=F32, tm=2048, tn=512, tk=d, name="hg_in").reshape(bp, seq, 4 * d)
    o_p, st_p = _hgrn_prompt(proj_p, lb, tt=256, hb=4)
    og_p = _gated_norm(o_p, proj_p, hg_g_norm, tt=256)
    xp = _mm(og_p.reshape(mp, d), hg_w_o, resid=xp.reshape(mp, d), gate=mods_p[1], gate_col=2,
             gate_group_rows=seq, out_dtype=F32, tm=1024, tn=512, tk=d, name="hg_o").reshape(bp, seq, d)

    hs = _prenorm(xs, norm_g4, 2, mods_s[1], 0, tt=bs)
    proj_s = _mm(hs.reshape(bs, d), hg_w_in, out_dtype=F32, tm=bs, tn=1024, tk=2048, name="hg_in_s")
    o_s, st_s = _hgrn_decode(proj_s.reshape(bs, 4, hg_heads, HG_KDIM), lb.reshape(hg_heads, HG_KDIM), state_hgrn[0])
    og_s = _gated_norm(o_s.reshape(1, bs, d), proj_s.reshape(1, bs, 4 * d), hg_g_norm, tt=bs)
    xs = _mm(og_s.reshape(bs, d), hg_w_o, resid=xs.reshape(bs, d), gate=mods_s[1], gate_col=2, out_dtype=F32,
             tm=bs, tn=1024, tk=2048, name="hg_o_s").reshape(1, bs, d)

    router = jnp.concatenate([moe_router[0], jnp.zeros((d, LANE - N_EXPERTS), F32)], axis=1)
    hp, route_p = _prenorm(xp, norm_g4, 3, mods_p[1], 3, router, tt=256)
    hs, route_s = _prenorm(xs, norm_g4, 3, mods_s[1], 3, router, tt=bs)
    h_all = jnp.concatenate([hp.reshape(mp, d // 2), hs.reshape(bs, d // 2)], axis=0)
    route = jnp.concatenate([route_p.reshape(mp, LANE), route_s.reshape(bs, LANE)], axis=0)
    n_tok = mp + bs
    y_sorted, dest2 = _experts(h_all, route, moe_w1[0], moe_w3[0], moe_w2[0], half=MOE_HALF, tn_up=512,
                               tn_down=2048, tk_down=512)
    y_tok_p = y_sorted.at[dest2[:, :mp]].get(mode="promise_in_bounds").reshape(TOP_K, bp, seq, d)
    y_tok_s = y_sorted.at[dest2[:, mp:]].get(mode="promise_in_bounds").reshape(TOP_K, 1, bs, d)
    w_tok = jnp.concatenate([route[:, TOP_K:2 * TOP_K], jnp.zeros((n_tok, LANE - TOP_K), F32)], axis=1)
    fn = final_norm.reshape(1, d)
    y_prompt = _moe_combine_final(xp, mods_p[1], 5, y_tok_p, w_tok[:mp].reshape(bp, seq, LANE), fn, tt=256)
    y_sample = _moe_combine_final(xs, mods_s[1], 5, y_tok_s, w_tok[mp:].reshape(1, bs, LANE), fn, tt=bs)

    return (y_prompt, y_sample.reshape(bs, dec_seq, d),
            ckv_p[None], kpe_p[None], st_p[None],
            ckv_s.reshape(1, bs, dec_seq, KV_LORA), kpe_s.reshape(1, bs, dec_seq, ROPE_DIM), st_s[None])


def _experts(h_all, route, w1, w3, w2, *, half, tn_up, tn_down, tk_down):
    n_tok = h_all.shape[0]
    n_pair = n_tok * TOP_K
    tile = 2 * half
    n_tile = pl.cdiv(n_pair + N_EXPERTS * (tile - 1), tile)
    n_half = 2 * n_tile
    r_rows = n_tile * tile
    e_pair = route[:, :TOP_K].astype(I32).reshape(n_pair)
    onehot = (e_pair[:, None] == jnp.arange(N_EXPERTS, dtype=I32)[None, :]).astype(I32)
    counts = jnp.sum(onehot, axis=0)
    rank = jnp.sum(onehot * (jnp.cumsum(onehot, axis=0) - onehot), axis=1)
    padded = ((counts + tile - 1) // tile) * tile
    ends = jnp.cumsum(padded)
    starts = ends - padded
    dest = jnp.sum(onehot * starts[None, :], axis=1) + rank
    src_token = (jnp.arange(r_rows, dtype=I32) % n_tok).at[dest].set(jnp.arange(n_pair, dtype=I32) // TOP_K)
    half_start = jnp.arange(n_half, dtype=I32) * half
    half_e = jnp.minimum(jnp.sum((ends[None, :] <= half_start[:, None]).astype(I32), axis=1), N_EXPERTS - 1)
    half_oh = (half_e[:, None] == jnp.arange(N_EXPERTS, dtype=I32)[None, :]).astype(I32)
    half_occ = half_start < jnp.sum(half_oh * (starts + counts)[None, :], axis=1)

    def occupied_list(occ):
        n = occ.shape[0]
        occ_i = occ.astype(I32)
        n_occ = jnp.sum(occ_i)
        pos = jnp.where(occ, jnp.cumsum(occ_i) - 1, n)
        lst = jnp.zeros((n + 1,), I32).at[pos].set(jnp.arange(n, dtype=I32))[:n]
        return lst[jnp.minimum(jnp.arange(n, dtype=I32), n_occ - 1)], n_occ.reshape(1)

    half_list, n_half_occ = occupied_list(half_occ)
    tile_list, n_tile_occ = occupied_list(half_occ[0::2])
    x_sorted = h_all.at[src_token].get(mode="promise_in_bounds")
    act = _moe_up(half_list, half_e[half_list], n_half_occ, x_sorted, w1, w3, tm=half, tn=tn_up,
                  max_occupied=pl.cdiv(n_pair, half) + N_EXPERTS)
    y_sorted = _moe_down(tile_list, half_e[2 * tile_list], half_occ[2 * tile_list + 1].astype(I32), n_tile_occ,
                         act, w2, tm=tile, tn=tn_down, tk=tk_down)
    return y_sorted, dest.reshape(n_tok, TOP_K).T
```

```python
import functools

import jax
import jax.numpy as jnp
from jax import lax
from jax.experimental import pallas as pl
from jax.experimental.pallas import tpu as pltpu

F32 = jnp.float32
BF16 = jnp.bfloat16
I32 = jnp.int32

PAGE_SIZE = 128
MLA_HEADS = 32
Q_LORA = 1024
KV_LORA = 512
NOPE_DIM = 128
ROPE_DIM = 64
V_DIM = 128
ROPE_THETA = 10000.0
ATTN_SCALE = (NOPE_DIM + ROPE_DIM) ** -0.5
HG_KDIM = 128
HG_CHUNK = 64
N_EXPERTS = 8
TOP_K = 2
N_ADA = 6
EPS = 1e-6
NEG_INF = -1e30

LANE = 128
MIB = 1024 * 1024
VMEM_CAP_BYTES = 56 * MIB


def _params(n_grid, vmem_mib):
    return pltpu.CompilerParams(
        dimension_semantics=("arbitrary",) * n_grid,
        vmem_limit_bytes=min(vmem_mib * MIB, VMEM_CAP_BYTES))


def _silu(x):
    return x * jax.nn.sigmoid(x)


def _split3(x):
    hi = x.astype(BF16)
    r1 = x - hi.astype(F32)
    mid = r1.astype(BF16)
    lo = (r1 - mid.astype(F32)).astype(BF16)
    return hi, mid, lo


def _mm_kernel(*refs, nk, tk, k_valid, a_silu, n_w, epi):
    it = iter(refs)
    a_ref = next(it)
    w_ref = next(it)
    w3_ref = next(it) if n_w == 2 else None
    bias_ref = next(it) if epi == "bias" else None
    x_ref = next(it) if epi == "resid" else None
    gate_ref = next(it) if epi == "resid" else None
    o_ref = next(it)
    acc_ref = next(it) if nk > 1 else None
    acc3_ref = next(it) if (nk > 1 and n_w == 2) else None

    k = pl.program_id(2)
    ragged = (k_valid % tk) != 0
    w_refs = [w_ref, w3_ref][:n_w]

    def dots(masked):
        a = a_ref[...]
        if a_silu:
            a = _silu(a.astype(F32))
        a = a.astype(BF16)
        if masked:
            col = k * tk + lax.broadcasted_iota(I32, a.shape, 1)
            a = jnp.where(col < k_valid, a, jnp.zeros_like(a))
        out = []
        for ref in w_refs:
            w = ref[...]
            if masked:
                row = k * tk + lax.broadcasted_iota(I32, w.shape, 0)
                w = jnp.where(row < k_valid, w, jnp.zeros_like(w))
            out.append(jnp.dot(a, w.astype(BF16), preferred_element_type=F32))
        return out + [None] * (2 - n_w)

    def epilogue(h, h3):
        if epi == "plain":
            o_ref[...] = h.astype(o_ref.dtype)
        elif epi == "bias":
            o_ref[...] = (h + bias_ref[...]).astype(o_ref.dtype)
        elif epi == "swiglu":
            o_ref[...] = (_silu(h) * h3).astype(o_ref.dtype)
        else:
            o_ref[...] = (x_ref[...] + gate_ref[...] * h).astype(o_ref.dtype)

    if nk == 1:
        epilogue(*dots(ragged))
        return

    @pl.when(k == 0)
    def _():
        d, d3 = dots(False)
        acc_ref[...] = d
        if n_w == 2:
            acc3_ref[...] = d3

    if nk > 2:
        @pl.when((k > 0) & (k < nk - 1))
        def _():
            d, d3 = dots(False)
            acc_ref[...] += d
            if n_w == 2:
                acc3_ref[...] += d3

    @pl.when(k == nk - 1)
    def _():
        d, d3 = dots(ragged)
        epilogue(acc_ref[...] + d, acc3_ref[...] + d3 if n_w == 2 else None)


def _mm(a, w, *, layer=0, w3=None, bias=None, resid=None, gate=None, gate_col=0, gate_group_rows=None,
        out_dtype=F32, tm, tn, tk, a_silu=False, name="mm"):
    m, kdim = a.shape
    n = w.shape[-1]
    tm = min(tm, m)
    tn = min(tn, n)
    tk = min(tk, kdim)
    nk = pl.cdiv(kdim, tk)
    grid = (pl.cdiv(m, tm), pl.cdiv(n, tn), nk)
    n_w = 2 if w3 is not None else 1
    epi = "swiglu" if w3 is not None else "bias" if bias is not None else "resid" if resid is not None else "plain"

    a_bufs = 1 if (nk == 1 and grid[1] > 1) else 2
    a_mode = dict(pipeline_mode=pl.Buffered(1)) if a_bufs == 1 else {}
    in_specs = [pl.BlockSpec((tm, tk), lambda i, j, k: (i, k), **a_mode),
                pl.BlockSpec((None, tk, tn), lambda i, j, k: (layer, k, j))]
    args = [a, w]
    if n_w == 2:
        in_specs.append(pl.BlockSpec((None, tk, tn), lambda i, j, k: (layer, k, j)))
        args.append(w3)
    if epi == "bias":
        in_specs.append(pl.BlockSpec((None, 1, tn), lambda i, j, k: (layer, 0, j)))
        args.append(bias)
    if epi == "resid":
        in_specs.append(pl.BlockSpec((tm, tn), lambda i, j, k: (i, j)))
        args.append(resid)
        r = gate.shape[1]
        nj = n // tn
        if r == 1:
            per = gate_group_rows // tm
            in_specs.append(pl.BlockSpec((None, 1, tn), lambda i, j, k: (i // per, 0, gate_col * nj + j)))
        else:
            in_specs.append(pl.BlockSpec((None, tm, tn), lambda i, j, k: (i, 0, gate_col * nj + j)))
        args.append(gate)

    scratch = []
    if nk > 1:
        scratch = [pltpu.VMEM((tm, tn), F32)] * n_w
    out_b = jnp.dtype(out_dtype).itemsize
    est = (a_bufs * tm * tk * a.dtype.itemsize + 2 * n_w * tk * tn * 4 + n_w * tk * tn * 2
           + (n_w + 1) * tm * tn * 4 + 2 * tm * tn * out_b + (3 * tm * tn * 4 if epi == "resid" else 0))
    kern = functools.partial(_mm_kernel, nk=nk, tk=tk, k_valid=kdim, a_silu=a_silu, n_w=n_w, epi=epi)
    return pl.pallas_call(
        kern,
        out_shape=jax.ShapeDtypeStruct((m, n), out_dtype),
        grid=grid,
        in_specs=in_specs,
        out_specs=pl.BlockSpec((tm, tn), lambda i, j, k: (i, j)),
        scratch_shapes=scratch,
        compiler_params=_params(3, est // MIB + 8),
        name=name,
    )(*args)


def _prenorm_kernel(*refs, with_router):
    if with_router:
        x_ref, g_ref, sh_ref, sc_ref, r_ref, o_ref, route_ref = refs
    else:
        x_ref, g_ref, sh_ref, sc_ref, o_ref = refs
    x = x_ref[...]
    y = x * lax.rsqrt(jnp.mean(x * x, axis=-1, keepdims=True) + EPS) * g_ref[...]
    h = y * (1.0 + sc_ref[...]) + sh_ref[...]
    if not with_router:
        o_ref[...] = h.astype(o_ref.dtype)
    else:
        half = h.shape[-1] // 2
        bits = lax.bitcast_convert_type(h.astype(BF16).astype(F32), jnp.uint32)
        o_ref[...] = bits[:, half:] | (bits[:, :half] >> 16)
        h1, h2, _ = _split3(h)
        r = r_ref[...]
        r1, r2, _ = _split3(r)
        logits = (jnp.dot(h1, r1, preferred_element_type=F32)
                  + jnp.dot(h1, r2, preferred_element_type=F32)
                  + jnp.dot(h2, r1, preferred_element_type=F32))
        lane = lax.broadcasted_iota(I32, logits.shape, 1).astype(F32)
        logits = jnp.where(lane < N_EXPERTS, logits, -jnp.inf)
        v1 = jnp.max(logits, axis=-1, keepdims=True)
        i1 = jnp.min(jnp.where(logits == v1, lane, float(LANE)), axis=-1, keepdims=True)
        rest = jnp.where(lane == i1, -jnp.inf, logits)
        v2 = jnp.max(rest, axis=-1, keepdims=True)
        i2 = jnp.min(jnp.where(rest == v2, lane, float(LANE)), axis=-1, keepdims=True)
        e2 = jnp.exp(v2 - v1)
        w1 = 1.0 / (1.0 + e2)
        w2 = e2 / (1.0 + e2)
        route = jnp.where(lane == 0, i1, jnp.where(lane == 1, i2, jnp.where(lane == 2, w1, jnp.where(lane == 3, w2, 0.0))))
        route_ref[...] = route


def _prenorm(x3, norm_g4, g_idx, mod, shift_col, router=None, *, tt):
    b, t, d = x3.shape
    tt = min(tt, t)
    r = mod.shape[1]
    if r == 1:
        mod_spec = lambda c: pl.BlockSpec((None, 1, d), lambda bi, ti: (bi, 0, c))
    else:
        mod_spec = lambda c: pl.BlockSpec((None, tt, d), lambda bi, ti: (bi, ti, c))
    in_specs = [pl.BlockSpec((None, tt, d), lambda bi, ti: (bi, ti, 0)),
                pl.BlockSpec((None, 1, d), lambda bi, ti: (g_idx, 0, 0)),
                mod_spec(shift_col), mod_spec(shift_col + 1)]
    args = [x3, norm_g4, mod, mod]
    if router is None:
        out_shape = [jax.ShapeDtypeStruct((b, t, d), BF16)]
        out_specs = [pl.BlockSpec((None, tt, d), lambda bi, ti: (bi, ti, 0))]
    else:
        out_shape = [jax.ShapeDtypeStruct((b, t, d // 2), jnp.uint32)]
        out_specs = [pl.BlockSpec((None, tt, d // 2), lambda bi, ti: (bi, ti, 0))]
    if router is not None:
        in_specs.append(pl.BlockSpec((d, LANE), lambda bi, ti: (0, 0)))
        args.append(router)
        out_shape.append(jax.ShapeDtypeStruct((b, t, LANE), F32))
        out_specs.append(pl.BlockSpec((None, tt, LANE), lambda bi, ti: (bi, ti, 0)))
    res = pl.pallas_call(
        functools.partial(_prenorm_kernel, with_router=router is not None),
        out_shape=out_shape, grid=(b, t // tt), in_specs=in_specs, out_specs=out_specs,
        compiler_params=_params(2, 48), name="prenorm_router" if router is not None else "prenorm",
    )(*args)
    return res if router is not None else res[0]


def _mla_post_kernel(p_ref, qn_ref, kn_ref, cos_ref, sin_ref, cq_ref, ckv_ref, kpe_ref):
    cq = p_ref[:, :Q_LORA]
    cq = cq * lax.rsqrt(jnp.mean(cq * cq, axis=-1, keepdims=True) + EPS) * qn_ref[...]
    cq_ref[...] = cq.astype(cq_ref.dtype)
    c = p_ref[:, Q_LORA:Q_LORA + KV_LORA]
    ckv_ref[...] = c * lax.rsqrt(jnp.mean(c * c, axis=-1, keepdims=True) + EPS) * kn_ref[...]
    slab = p_ref[:, Q_LORA + KV_LORA:]
    rot = pltpu.roll(slab, ROPE_DIM, 1)
    kpe = slab * cos_ref[...] + rot * sin_ref[...]
    kpe_ref[...] = kpe[:, :ROPE_DIM]


def _mla_post(proj3, q_norm, kv_norm, cos_p, sin_p, *, tt):
    b, t, n = proj3.shape
    tt = min(tt, t)
    return pl.pallas_call(
        _mla_post_kernel,
        out_shape=[jax.ShapeDtypeStruct((b, t, Q_LORA), BF16),
                   jax.ShapeDtypeStruct((b, t, KV_LORA), F32),
                   jax.ShapeDtypeStruct((b, t, ROPE_DIM), F32)],
        grid=(b, t // tt),
        in_specs=[pl.BlockSpec((None, tt, n), lambda bi, ti: (bi, ti, 0)),
                  pl.BlockSpec((1, Q_LORA), lambda bi, ti: (0, 0)),
                  pl.BlockSpec((1, KV_LORA), lambda bi, ti: (0, 0)),
                  pl.BlockSpec((tt, LANE), lambda bi, ti: (ti, 0)),
                  pl.BlockSpec((tt, LANE), lambda bi, ti: (ti, 0))],
        out_specs=[pl.BlockSpec((None, tt, Q_LORA), lambda bi, ti: (bi, ti, 0)),
                   pl.BlockSpec((None, tt, KV_LORA), lambda bi, ti: (bi, ti, 0)),
                   pl.BlockSpec((None, tt, ROPE_DIM), lambda bi, ti: (bi, ti, 0))],
        compiler_params=_params(2, 32), name="mla_post",
    )(proj3, q_norm, kv_norm, cos_p, sin_p)


def _attn_prompt_kernel(q_ref, cos_ref, sin_ref, kt_ref, v_ref, wuk_ref, wuv_ref, o_ref,
                        qs_ref, m_ref, l_ref, acc_ref, *, tq, tkv):
    qi = pl.program_id(1)
    hn = MLA_HEADS * NOPE_DIM
    hr = MLA_HEADS * ROPE_DIM
    n_pair = MLA_HEADS // 2
    cos_t = jnp.concatenate([cos_ref[...]] * (hr // LANE), axis=1)
    sin_t = jnp.concatenate([sin_ref[...]] * (hr // LANE), axis=1)
    qpe = (q_ref[:, hn:hn + hr].astype(F32) * cos_t
           + q_ref[:, hn + hr:hn + 2 * hr].astype(F32) * sin_t).astype(BF16)
    n_full = (qi * tq) // tkv
    rows = n_pair * tq

    for par in range(2):
        for mi in range(n_pair):
            h = 2 * mi + par
            ql = jnp.dot(q_ref[:, h * NOPE_DIM:(h + 1) * NOPE_DIM], wuk_ref[h], preferred_element_type=F32)
            qs_ref[mi * tq:(mi + 1) * tq, :KV_LORA] = ql.astype(BF16)
            qs_ref[mi * tq:(mi + 1) * tq, KV_LORA:] = qpe[:, mi * LANE:(mi + 1) * LANE]
        m_ref[...] = jnp.full(m_ref.shape, -jnp.inf, F32)
        l_ref[...] = jnp.zeros(l_ref.shape, F32)
        acc_ref[...] = jnp.zeros(acc_ref.shape, F32)

        def step(j, masked):
            s = jnp.dot(qs_ref[...], kt_ref[par, j], preferred_element_type=F32) * ATTN_SCALE
            if masked:
                r = lax.broadcasted_iota(I32, (rows, tkv), 0)
                q_pos = qi * tq + (r & (tq - 1))
                k_pos = j * tkv + lax.broadcasted_iota(I32, (rows, tkv), 1)
                s = jnp.where(k_pos <= q_pos, s, NEG_INF)
            m_prev = m_ref[...]
            m_new = jnp.maximum(m_prev, jnp.max(s, axis=1, keepdims=True))
            alpha = jnp.exp(m_prev - m_new)
            p = jnp.exp(s - jnp.tile(m_new, (1, tkv // LANE)))
            l_ref[...] = alpha * l_ref[...] + jnp.sum(p, axis=1, keepdims=True)
            pv = jnp.dot(p.astype(BF16), v_ref[j], preferred_element_type=F32)
            acc_ref[...] = acc_ref[...] * jnp.tile(alpha, (1, KV_LORA // LANE)) + pv
            m_ref[...] = m_new

        def body(j, carry):
            step(j, False)
            return carry

        lax.fori_loop(0, n_full, body, 0)
        step(n_full, True)

        inv = 1.0 / l_ref[...]
        for mi in range(n_pair):
            h = 2 * mi + par
            sl = slice(mi * tq, (mi + 1) * tq)
            lat = (acc_ref[sl, :] * jnp.tile(inv[sl, :], (1, KV_LORA // LANE))).astype(BF16)
            o_ref[:, h * V_DIM:(h + 1) * V_DIM] = jnp.dot(
                lat, wuv_ref[h], preferred_element_type=F32).astype(o_ref.dtype)


def _attn_prompt(q3, cos128, sin128, kt, v4, wuk, wuv, *, tq, tkv):
    b, t, nq = q3.shape
    nkv = t // tkv
    kc = KV_LORA + LANE
    rows = (MLA_HEADS // 2) * tq
    one = pl.Buffered(1)
    return pl.pallas_call(
        functools.partial(_attn_prompt_kernel, tq=tq, tkv=tkv),
        out_shape=jax.ShapeDtypeStruct((b, t, MLA_HEADS * V_DIM), BF16),
        grid=(b, t // tq),
        in_specs=[pl.BlockSpec((None, tq, nq), lambda bi, qi: (bi, qi, 0)),
                  pl.BlockSpec((tq, LANE), lambda bi, qi: (qi, 0)),
                  pl.BlockSpec((tq, LANE), lambda bi, qi: (qi, 0)),
                  pl.BlockSpec((None, 2, nkv, kc, tkv), lambda bi, qi: (bi, 0, 0, 0, 0), pipeline_mode=one),
                  pl.BlockSpec((None, nkv, tkv, KV_LORA), lambda bi, qi: (bi, 0, 0, 0), pipeline_mode=one),
                  pl.BlockSpec((MLA_HEADS, NOPE_DIM, KV_LORA), lambda bi, qi: (0, 0, 0), pipeline_mode=one),
                  pl.BlockSpec((MLA_HEADS, KV_LORA, V_DIM), lambda bi, qi: (0, 0, 0), pipeline_mode=one)],
        out_specs=pl.BlockSpec((None, tq, MLA_HEADS * V_DIM), lambda bi, qi: (bi, qi, 0)),
        scratch_shapes=[pltpu.VMEM((rows, kc), BF16), pltpu.VMEM((rows, LANE), F32),
                        pltpu.VMEM((rows, LANE), F32), pltpu.VMEM((rows, KV_LORA), F32)],
        compiler_params=_params(2, 56), name="mla_attn_prompt",
    )(q3, cos128, sin128, kt, v4, wuk, wuv)


def _head_mm_kernel(a_ref, w_ref, o_ref):
    o_ref[...] = jnp.dot(a_ref[...].astype(BF16), w_ref[...], preferred_element_type=F32).astype(o_ref.dtype)


def _head_mm(a, w, ka, na):
    m = a.shape[0]
    h = w.shape[0]
    return pl.pallas_call(
        _head_mm_kernel,
        out_shape=jax.ShapeDtypeStruct((m, h * na), BF16),
        grid=(h,),
        in_specs=[pl.BlockSpec((m, ka), lambda hi: (0, hi)),
                  pl.BlockSpec((None, ka, na), lambda hi: (hi, 0, 0))],
        out_specs=pl.BlockSpec((m, na), lambda hi: (0, hi)),
        compiler_params=_params(1, 16), name="head_mm",
    )(a, w)


def _rope_q_kernel(x_ref, xr_ref, cos_ref, sin_ref, o_ref):
    o_ref[...] = (x_ref[...].astype(F32) * cos_ref[...] + xr_ref[...].astype(F32) * sin_ref[...]).astype(o_ref.dtype)


def _rope_q(q, cos_t, sin_t):
    m = q.shape[0]
    hr = MLA_HEADS * ROPE_DIM
    nope_blocks = (MLA_HEADS * NOPE_DIM) // hr
    return pl.pallas_call(
        _rope_q_kernel,
        out_shape=jax.ShapeDtypeStruct((m, hr), BF16),
        grid=(1,),
        in_specs=[pl.BlockSpec((m, hr), lambda i: (0, nope_blocks)),
                  pl.BlockSpec((m, hr), lambda i: (0, nope_blocks + 1)),
                  pl.BlockSpec((m, hr), lambda i: (0, 0)),
                  pl.BlockSpec((m, hr), lambda i: (0, 0))],
        out_specs=pl.BlockSpec((m, hr), lambda i: (0, 0)),
        compiler_params=_params(1, 16), name="rope_q",
    )(q, q, cos_t, sin_t)


def _attn_decode_kernel(pt_ref, ql_ref, qp_ref, cn_ref, kn_ref, *rest, ns, pps, n_steps):
    n_pg = ns * pps
    ckv_refs = rest[:n_pg]
    kpe_refs = rest[n_pg:2 * n_pg]
    o_ref = rest[2 * n_pg]
    m_ref, l_ref, acc_ref, kbuf_ref, pbuf_ref = rest[2 * n_pg + 1:]
    c = pl.program_id(1)
    nt = (((1,), (1,)), ((), ()))

    for si in range(ns):
        for p in range(pps):
            kbuf_ref[si, p * PAGE_SIZE:(p + 1) * PAGE_SIZE, :] = ckv_refs[si * pps + p][...].astype(BF16)
            pbuf_ref[si, :, p * PAGE_SIZE:(p + 1) * PAGE_SIZE] = kpe_refs[si * pps + p][...].astype(BF16)

    @pl.when(c == 0)
    def _():
        for si in range(ns):
            cn = cn_ref[si]
            kn = kn_ref[si]
            s_new = (jnp.sum(ql_ref[si].astype(F32) * cn, axis=-1, keepdims=True)
                     + jnp.sum(qp_ref[si].astype(F32) * kn, axis=-1, keepdims=True)) * ATTN_SCALE
            m_ref[si] = jnp.broadcast_to(s_new, m_ref.shape[1:])
            l_ref[si] = jnp.ones(l_ref.shape[1:], F32)
            acc_ref[si] = jnp.broadcast_to(cn, acc_ref.shape[1:])

    for si in range(ns):
        kb = kbuf_ref[si]
        s = (lax.dot_general(ql_ref[si], kb, nt, preferred_element_type=F32)
             + jnp.dot(qp_ref[si], pbuf_ref[si], preferred_element_type=F32)) * ATTN_SCALE
        m_prev = m_ref[si]
        m_new = jnp.maximum(m_prev, jnp.max(s, axis=1, keepdims=True))
        alpha = jnp.exp(m_prev - m_new)
        p_all = jnp.exp(s - jnp.tile(m_new, (1, pps))).astype(BF16)
        l_ref[si] = alpha * l_ref[si] + jnp.sum(p_all.astype(F32), axis=1, keepdims=True)
        pv = jnp.dot(p_all, kb, preferred_element_type=F32)
        acc_ref[si] = acc_ref[si] * jnp.tile(alpha, (1, KV_LORA // LANE)) + pv
        m_ref[si] = m_new

    @pl.when(c == n_steps - 1)
    def _():
        for si in range(ns):
            o_ref[si] = (acc_ref[si] / jnp.tile(l_ref[si], (1, KV_LORA // LANE))).astype(o_ref.dtype)


def _attn_decode(page_table, ql3, qp3, ckv_new3, kpe_new3, cache_ckv, cache_kpe_t, layer, *, ns, pps):
    nb, n_pages = page_table.shape
    n_steps = n_pages // pps
    h = MLA_HEADS

    def page_spec(rows, width, si, p):
        return pl.BlockSpec((None, None, rows, width),
                            lambda bi, ci, pt: (layer, pt[bi * ns + si, ci * pps + p], 0, 0))

    in_specs = [pl.BlockSpec((ns, h, KV_LORA), lambda bi, ci, pt: (bi, 0, 0)),
                pl.BlockSpec((ns, h, ROPE_DIM), lambda bi, ci, pt: (bi, 0, 0)),
                pl.BlockSpec((ns, 1, KV_LORA), lambda bi, ci, pt: (bi, 0, 0)),
                pl.BlockSpec((ns, 1, ROPE_DIM), lambda bi, ci, pt: (bi, 0, 0))]
    in_specs += [page_spec(PAGE_SIZE, KV_LORA, si, p) for si in range(ns) for p in range(pps)]
    in_specs += [page_spec(ROPE_DIM, PAGE_SIZE, si, p) for si in range(ns) for p in range(pps)]
    grid_spec = pltpu.PrefetchScalarGridSpec(
        num_scalar_prefetch=1, grid=(nb // ns, n_steps), in_specs=in_specs,
        out_specs=pl.BlockSpec((ns, h, KV_LORA), lambda bi, ci, pt: (bi, 0, 0)),
        scratch_shapes=[pltpu.VMEM((ns, h, LANE), F32), pltpu.VMEM((ns, h, LANE), F32),
                        pltpu.VMEM((ns, h, KV_LORA), F32),
                        pltpu.VMEM((ns, pps * PAGE_SIZE, KV_LORA), BF16),
                        pltpu.VMEM((ns, ROPE_DIM, pps * PAGE_SIZE), BF16)])
    n_pg = ns * pps
    return pl.pallas_call(
        functools.partial(_attn_decode_kernel, ns=ns, pps=pps, n_steps=n_steps),
        out_shape=jax.ShapeDtypeStruct((nb, h, KV_LORA), BF16),
        grid_spec=grid_spec,
        compiler_params=_params(2, 40), name="mla_attn_decode",
    )(page_table, ql3, qp3, ckv_new3, kpe_new3, *([cache_ckv] * n_pg), *([cache_kpe_t] * n_pg))


def _hgrn_prompt_kernel(q_ref, f_ref, i_ref, lb_ref, o_ref, s_out_ref, s_ref, *, tt, hb, n_tb):
    tb = pl.program_id(2)

    @pl.when(tb == 0)
    def _():
        s_ref[...] = jnp.zeros(s_ref.shape, F32)

    c = HG_CHUNK
    dk = HG_KDIM
    lb = lb_ref[...]
    sg = jax.nn.sigmoid(f_ref[...])
    log_f = jnp.log(lb + (1.0 - lb) * sg)
    kk = (1.0 - lb) * (1.0 - sg)
    qs = _silu(q_ref[...])
    vv = i_ref[...]
    row = lax.broadcasted_iota(I32, (c, c), 0)
    col = lax.broadcasted_iota(I32, (c, c), 1)
    causal = col <= row
    tril = causal.astype(BF16)
    mid = (c - 1) // 2
    nt = (((1,), (1,)), ((), ()))

    for ci in range(tt // c):
        rs = slice(ci * c, (ci + 1) * c)
        lf1, lf2, lf3 = _split3(log_f[rs])
        cum = (jnp.dot(tril, lf1, preferred_element_type=F32)
               + jnp.dot(tril, lf2, preferred_element_type=F32)
               + jnp.dot(tril, lf3, preferred_element_type=F32))
        ref = cum[mid:mid + 1]
        last = cum[c - 1:c]
        qc = qs[rs]
        kc = kk[rs]
        vb = vv[rs].astype(BF16)
        qd = (qc * jnp.exp(cum - ref)).astype(BF16)
        kd = (kc * jnp.exp(ref - cum)).astype(BF16)
        q0 = (qc * jnp.exp(cum)).astype(BF16)
        k2 = kc * jnp.exp(last - cum)
        dec = jnp.exp(last)
        for h in range(hb):
            ls = slice(h * dk, (h + 1) * dk)
            a = lax.dot_general(qd[:, ls], kd[:, ls], nt, preferred_element_type=F32)
            a = jnp.where(causal, a, 0.0).astype(BF16)
            s_prev = s_ref[h]
            o = (jnp.dot(q0[:, ls], s_prev.astype(BF16), preferred_element_type=F32)
                 + jnp.dot(a, vb[:, ls], preferred_element_type=F32))
            o_ref[rs, ls] = o
            dec_col = jnp.broadcast_to(dec[:, ls], (dk, dk)).T
            k2t = k2[:, ls].T.astype(BF16)
            s_ref[h] = dec_col * s_prev + jnp.dot(k2t, vb[:, ls], preferred_element_type=F32)

    @pl.when(tb == n_tb - 1)
    def _():
        s_out_ref[...] = s_ref[...]


def _hgrn_prompt(proj3, lb2, *, tt, hb):
    b, t, n4 = proj3.shape
    d = n4 // 4
    heads = d // HG_KDIM
    w = hb * HG_KDIM
    nhg = heads // hb
    n_tb = t // tt
    return pl.pallas_call(
        functools.partial(_hgrn_prompt_kernel, tt=tt, hb=hb, n_tb=n_tb),
        out_shape=[jax.ShapeDtypeStruct((b, t, d), F32),
                   jax.ShapeDtypeStruct((b, heads, HG_KDIM, HG_KDIM), F32)],
        grid=(b, nhg, n_tb),
        in_specs=[pl.BlockSpec((None, tt, w), lambda bi, hi, ti: (bi, ti, hi)),
                  pl.BlockSpec((None, tt, w), lambda bi, hi, ti: (bi, ti, nhg + hi)),
                  pl.BlockSpec((None, tt, w), lambda bi, hi, ti: (bi, ti, 2 * nhg + hi)),
                  pl.BlockSpec((1, w), lambda bi, hi, ti: (0, hi))],
        out_specs=[pl.BlockSpec((None, tt, w), lambda bi, hi, ti: (bi, ti, hi)),
                   pl.BlockSpec((None, hb, HG_KDIM, HG_KDIM), lambda bi, hi, ti: (bi, hi, 0, 0))],
        scratch_shapes=[pltpu.VMEM((hb, HG_KDIM, HG_KDIM), F32)],
        compiler_params=_params(3, 32), name="hgrn_prompt",
    )(proj3, proj3, proj3, lb2)


def _hgrn_decode_kernel(p_ref, lb_ref, s_ref, o_ref, s_out_ref):
    heads = s_ref.shape[0]
    dk = HG_KDIM
    lb = lb_ref[...]
    sg = jax.nn.sigmoid(p_ref[1])
    fg = lb + (1.0 - lb) * sg
    kk = (1.0 - lb) * (1.0 - sg)
    qs = _silu(p_ref[0])
    vv = p_ref[2]
    outs = []
    for h in range(heads):
        f_col = jnp.broadcast_to(fg[h:h + 1], (dk, dk)).T
        k_col = jnp.broadcast_to(kk[h:h + 1], (dk, dk)).T
        q_col = jnp.broadcast_to(qs[h:h + 1], (dk, dk)).T
        s_new = f_col * s_ref[h] + k_col * vv[h:h + 1]
        s_out_ref[h] = s_new
        outs.append(jnp.sum(q_col * s_new, axis=0, keepdims=True))
    o_ref[...] = jnp.concatenate(outs, axis=0)


def _hgrn_decode(proj4, lb2, state):
    b, _, heads, dk = proj4.shape
    return pl.pallas_call(
        _hgrn_decode_kernel,
        out_shape=[jax.ShapeDtypeStruct((b, heads, dk), F32),
                   jax.ShapeDtypeStruct(state.shape, F32)],
        grid=(b,),
        in_specs=[pl.BlockSpec((None, 4, heads, dk), lambda bi: (bi, 0, 0, 0)),
                  pl.BlockSpec((heads, dk), lambda bi: (0, 0)),
                  pl.BlockSpec((None, heads, dk, dk), lambda bi: (bi, 0, 0, 0))],
        out_specs=[pl.BlockSpec((None, heads, dk), lambda bi: (bi, 0, 0)),
                   pl.BlockSpec((None, heads, dk, dk), lambda bi: (bi, 0, 0, 0))],
        compiler_params=_params(1, 32), name="hgrn_decode",
    )(proj4, lb2, state)


def _gated_norm_kernel(o_ref, g_ref, gn_ref, out_ref):
    o = o_ref[...]
    y = o * lax.rsqrt(jnp.mean(o * o, axis=-1, keepdims=True) + EPS) * gn_ref[...]
    out_ref[...] = (y * _silu(g_ref[...])).astype(out_ref.dtype)


def _gated_norm(o3, proj3, g_norm, *, tt):
    b, t, d = o3.shape
    tt = min(tt, t)
    return pl.pallas_call(
        _gated_norm_kernel,
        out_shape=jax.ShapeDtypeStruct((b, t, d), BF16),
        grid=(b, t // tt),
        in_specs=[pl.BlockSpec((None, tt, d), lambda bi, ti: (bi, ti, 0)),
                  pl.BlockSpec((None, tt, d), lambda bi, ti: (bi, ti, 3)),
                  pl.BlockSpec((1, d), lambda bi, ti: (0, 0))],
        out_specs=pl.BlockSpec((None, tt, d), lambda bi, ti: (bi, ti, 0)),
        compiler_params=_params(2, 48), name="gated_norm",
    )(o3, proj3, g_norm)


def _moe_up_kernel(hl_ref, he_ref, nv_ref, a_ref, w1_ref, w3_ref, o_ref):
    i = pl.program_id(1)

    @pl.when(i < nv_ref[0])
    def _():
        words = a_ref[...]
        half = words.shape[1]
        a_lo = lax.bitcast_convert_type(words << 16, F32).astype(BF16)
        a_hi = lax.bitcast_convert_type(words & jnp.uint32(0xFFFF0000), F32).astype(BF16)

        def proj(w_ref):
            w = w_ref[...].astype(BF16)
            return (jnp.dot(a_lo, w[:half], preferred_element_type=F32)
                    + jnp.dot(a_hi, w[half:], preferred_element_type=F32))

        o_ref[...] = (_silu(proj(w1_ref)) * proj(w3_ref)).astype(o_ref.dtype)


def _moe_up(half_list, half_expert, n_occupied, xs, w1, w3, *, tm, tn, max_occupied):
    r, dh = xs.shape
    d = 2 * dh
    n = w1.shape[-1]
    tn = min(tn, n)
    grid_spec = pltpu.PrefetchScalarGridSpec(
        num_scalar_prefetch=3, grid=(n // tn, min(r // tm, max_occupied)),
        in_specs=[pl.BlockSpec((tm, dh), lambda j, i, hl, he, nv: (hl[i], 0)),
                  pl.BlockSpec((None, d, tn), lambda j, i, hl, he, nv: (he[i], 0, j)),
                  pl.BlockSpec((None, d, tn), lambda j, i, hl, he, nv: (he[i], 0, j))],
        out_specs=pl.BlockSpec((tm, tn), lambda j, i, hl, he, nv: (hl[i], j)))
    est = 2 * tm * dh * 4 + 2 * tm * d * 2 + 4 * d * tn * 4 + 2 * d * tn * 2 + 2 * tm * tn * 2 + 3 * tm * tn * 4
    return pl.pallas_call(
        _moe_up_kernel,
        out_shape=jax.ShapeDtypeStruct((r, n), BF16),
        grid_spec=grid_spec,
        compiler_params=_params(2, est // MIB + 4), name="moe_up",
    )(half_list, half_expert, n_occupied, xs, w1, w3)


def _moe_down_kernel(tl_ref, te_ref, sv_ref, nv_ref, a_ref, w_ref, o_ref, *, half):
    i = pl.program_id(0)
    k = pl.program_id(2)
    valid = i < nv_ref[0]
    both = valid & (sv_ref[i] != 0)
    first_only = valid & (sv_ref[i] == 0)

    @pl.when(both & (k == 0))
    def _():
        o_ref[...] = jnp.dot(a_ref[...], w_ref[...].astype(BF16), preferred_element_type=F32)

    @pl.when(both & (k > 0))
    def _():
        o_ref[...] += jnp.dot(a_ref[...], w_ref[...].astype(BF16), preferred_element_type=F32)

    @pl.when(first_only & (k == 0))
    def _():
        o_ref[:half, :] = jnp.dot(a_ref[:half, :], w_ref[...].astype(BF16), preferred_element_type=F32)
        o_ref[half:, :] = jnp.zeros((o_ref.shape[0] - half, o_ref.shape[1]), o_ref.dtype)

    @pl.when(first_only & (k > 0))
    def _():
        o_ref[:half, :] += jnp.dot(a_ref[:half, :], w_ref[...].astype(BF16), preferred_element_type=F32)


def _moe_down(tile_list, tile_expert, second_valid, n_occupied, act, w2, *, tm, tn, tk):
    r, kdim = act.shape
    n = w2.shape[-1]
    tn = min(tn, n)
    tk = min(tk, kdim)
    nk = kdim // tk
    nn = n // tn

    def kk(i, k, nv):
        return jnp.where(i < nv[0], k, nk - 1)

    def jj(i, j, nv):
        return jnp.where(i < nv[0], j, nn - 1)

    grid_spec = pltpu.PrefetchScalarGridSpec(
        num_scalar_prefetch=4, grid=(r // tm, nn, nk),
        in_specs=[pl.BlockSpec((tm, tk), lambda i, j, k, tl, te, sv, nv: (tl[i], kk(i, k, nv))),
                  pl.BlockSpec((None, tk, tn), lambda i, j, k, tl, te, sv, nv: (te[i], kk(i, k, nv), jj(i, j, nv)))],
        out_specs=pl.BlockSpec((tm, tn), lambda i, j, k, tl, te, sv, nv: (tl[i], jj(i, j, nv))))
    est = 2 * tm * tk * 2 + 2 * tk * tn * 4 + tk * tn * 2 + 2 * tm * tn * 4 + tm * tn * 4
    return pl.pallas_call(
        functools.partial(_moe_down_kernel, half=tm // 2),
        out_shape=jax.ShapeDtypeStruct((r, n), F32),
        grid_spec=grid_spec,
        compiler_params=_params(3, est // MIB + 6), name="moe_down",
    )(tile_list, tile_expert, second_valid, n_occupied, act, w2)


def _moe_combine_kernel(x_ref, gate_ref, y_ref, w_ref, fn_ref, o_ref):
    w = w_ref[...]
    y = w[:, 0:1] * y_ref[0] + w[:, 1:2] * y_ref[1]
    x = x_ref[...] + gate_ref[...] * y
    o_ref[...] = x * lax.rsqrt(jnp.mean(x * x, axis=-1, keepdims=True) + EPS) * fn_ref[...]


def _moe_combine_final(x3, mod, gate_col, y4, w3, final_norm, *, tt):
    b, t, d = x3.shape
    tt = min(tt, t)
    r = mod.shape[1]
    if r == 1:
        gate_spec = pl.BlockSpec((None, 1, d), lambda bi, ti: (bi, 0, gate_col))
    else:
        gate_spec = pl.BlockSpec((None, tt, d), lambda bi, ti: (bi, ti, gate_col))
    return pl.pallas_call(
        _moe_combine_kernel,
        out_shape=jax.ShapeDtypeStruct((b, t, d), F32),
        grid=(b, t // tt),
        in_specs=[pl.BlockSpec((None, tt, d), lambda bi, ti: (bi, ti, 0)),
                  gate_spec,
                  pl.BlockSpec((TOP_K, None, tt, d), lambda bi, ti: (0, bi, ti, 0)),
                  pl.BlockSpec((None, tt, LANE), lambda bi, ti: (bi, ti, 0)),
                  pl.BlockSpec((1, d), lambda bi, ti: (0, 0))],
        out_specs=pl.BlockSpec((None, tt, d), lambda bi, ti: (bi, ti, 0)),
        compiler_params=_params(2, 48), name="moe_combine_final",
    )(x3, mod, y4, w3, final_norm)


MOE_HALF = 512


def kernel(x_prompt, x_sample, cache_ckv, cache_kpe, state_hgrn, page_table, c_prompt, c_sample,
           ada_w, ada_b, norm_g, mla_w_in, mla_q_norm, mla_w_uq, mla_kv_norm, mla_w_uk, mla_w_uv,
           mla_w_o, hg_w_in, hg_lower_bounds, hg_g_norm, hg_w_o, ffn_w1, ffn_w3, ffn_w2,
           moe_router, moe_w1, moe_w3, moe_w2, final_norm):
    bp, seq, d = x_prompt.shape
    bs, dec_seq, _ = x_sample.shape
    depth = ada_w.shape[0]
    assert depth == 2 and dec_seq == 1
    mp = bp * seq
    past_len = page_table.shape[1] * PAGE_SIZE
    heads = MLA_HEADS
    hg_heads = d // HG_KDIM

    n_c = bp + bs
    pad_c = (-n_c) % 8
    c_all = jnp.concatenate([c_prompt, c_sample, jnp.zeros((pad_c, d), F32)], axis=0)
    ada_b3 = ada_b.reshape(depth, 1, N_ADA * d)
    mods_p, mods_s = [], []
    for i in range(depth):
        mod = _mm(c_all, ada_w, layer=i, bias=ada_b3, out_dtype=F32, tm=n_c + pad_c, tn=1024, tk=2048,
                  a_silu=True, name="ada_mod")
        mods_p.append(mod[:bp].reshape(bp, 1, N_ADA * d))
        mods_s.append(mod[bp:bp + bs].reshape(1, bs, N_ADA * d))

    norm_g4 = norm_g.reshape(depth * 2, 1, d)
    xp = x_prompt
    xs = x_sample.reshape(1, bs, d)

    inv = ROPE_THETA ** (-jnp.arange(0, ROPE_DIM, 2, dtype=F32) / ROPE_DIM)

    def tables(pos):
        ang = pos.astype(F32)[:, None] * inv[None, :]
        cos2 = jnp.concatenate([jnp.cos(ang)] * 2, axis=1)
        sin2 = jnp.concatenate([jnp.sin(ang)] * 2, axis=1)
        return cos2, sin2

    cos_p, sin_p = tables(jnp.arange(seq))
    cos_s, sin_s = tables(jnp.broadcast_to(past_len + jnp.arange(dec_seq), (bs,)))
    zpad = lambda a: jnp.concatenate([a, jnp.zeros_like(a)], axis=1)
    dup = lambda a: jnp.concatenate([a, a], axis=1)

    w_in = mla_w_in[0]
    kpe_w = w_in[:, Q_LORA + KV_LORA:]
    half = ROPE_DIM // 2
    w_in_aug = jnp.concatenate([w_in, -kpe_w[:, half:], kpe_w[:, :half]], axis=1)[None]
    wq = mla_w_uq[0].reshape(Q_LORA, heads, NOPE_DIM + ROPE_DIM)
    wq_rope = wq[:, :, NOPE_DIM:]
    wq_rot = jnp.concatenate([-wq_rope[..., half:], wq_rope[..., :half]], axis=-1)
    w_uq_aug = jnp.concatenate([wq[:, :, :NOPE_DIM].reshape(Q_LORA, -1), wq_rope.reshape(Q_LORA, -1),
                                wq_rot.reshape(Q_LORA, -1)], axis=1)[None]
    wuk = jnp.transpose(mla_w_uk[0], (1, 2, 0)).astype(BF16)
    wuv = jnp.transpose(mla_w_uv[0], (1, 0, 2)).astype(BF16)
    q_norm = mla_q_norm[0].reshape(1, Q_LORA)
    kv_norm = mla_kv_norm[0].reshape(1, KV_LORA)

    hp = _prenorm(xp, norm_g4, 0, mods_p[0], 0, tt=256)
    proj_p = _mm(hp.reshape(mp, d), w_in_aug, out_dtype=F32, tm=1024, tn=2048, tk=1024, name="mla_in")
    cq_p, ckv_p, kpe_p = _mla_post(proj_p.reshape(bp, seq, -1), q_norm, kv_norm, zpad(cos_p), zpad(sin_p), tt=512)
    q_p = _mm(cq_p.reshape(mp, Q_LORA), w_uq_aug, out_dtype=BF16, tm=2048, tn=1024, tk=1024, name="mla_uq")
    tkv = 512
    nkv = seq // tkv
    ckv_b = ckv_p.astype(BF16)
    kpe_b = kpe_p.astype(BF16)
    zk = jnp.zeros_like(kpe_b)
    kcat = jnp.stack([jnp.concatenate([ckv_b, kpe_b, zk], axis=-1),
                      jnp.concatenate([ckv_b, zk, kpe_b], axis=-1)], axis=1)
    kt = jnp.swapaxes(kcat.reshape(bp, 2, nkv, tkv, KV_LORA + LANE), -1, -2)
    v4 = ckv_b.reshape(bp, nkv, tkv, KV_LORA)
    o_p = _attn_prompt(q_p.reshape(bp, seq, -1), dup(cos_p), dup(sin_p), kt, v4, wuk, wuv, tq=128, tkv=tkv)
    xp = _mm(o_p.reshape(mp, d), mla_w_o, resid=xp.reshape(mp, d), gate=mods_p[0], gate_col=2,
             gate_group_rows=seq, out_dtype=F32, tm=1024, tn=512, tk=d, name="mla_o").reshape(bp, seq, d)

    hs = _prenorm(xs, norm_g4, 0, mods_s[0], 0, tt=bs)
    proj_s = _mm(hs.reshape(bs, d), w_in_aug, out_dtype=F32, tm=bs, tn=2048, tk=2048, name="mla_in_s")
    cq_s, ckv_s, kpe_s = _mla_post(proj_s.reshape(1, bs, -1), q_norm, kv_norm, zpad(cos_s), zpad(sin_s), tt=bs)
    q_s = _mm(cq_s.reshape(bs, Q_LORA), w_uq_aug, out_dtype=BF16, tm=bs, tn=2048, tk=1024, name="mla_uq_s")
    ql_s = _head_mm(q_s, wuk, NOPE_DIM, KV_LORA).reshape(bs, heads, KV_LORA)
    qp_s = _rope_q(q_s, jnp.tile(cos_s, (1, heads)), jnp.tile(sin_s, (1, heads))).reshape(bs, heads, ROPE_DIM)
    lat_s = _attn_decode(page_table, ql_s, qp_s, ckv_s.reshape(bs, 1, KV_LORA), kpe_s.reshape(bs, 1, ROPE_DIM),
                         cache_ckv, jnp.swapaxes(cache_kpe, 2, 3), 0, ns=1, pps=32)
    o_s = _head_mm(lat_s.reshape(bs, heads * KV_LORA), wuv, KV_LORA, V_DIM)
    xs = _mm(o_s, mla_w_o, resid=xs.reshape(bs, d), gate=mods_s[0], gate_col=2, out_dtype=F32,
             tm=bs, tn=1024, tk=2048, name="mla_o_s").reshape(1, bs, d)

    hp = _prenorm(xp, norm_g4, 1, mods_p[0], 3, tt=256)
    act_p = _mm(hp.reshape(mp, d), ffn_w1, w3=ffn_w3, out_dtype=BF16, tm=2048, tn=256, tk=d, name="ffn_up")
    xp = _mm(act_p, ffn_w2, resid=xp.reshape(mp, d), gate=mods_p[0], gate_col=5, gate_group_rows=seq,
             out_dtype=F32, tm=1024, tn=1024, tk=1024, name="ffn_down").reshape(bp, seq, d)
    hs = _prenorm(xs, norm_g4, 1, mods_s[0], 3, tt=bs)
    act_s = _mm(hs.reshape(bs, d), ffn_w1, w3=ffn_w3, out_dtype=BF16, tm=bs, tn=1024, tk=1024, name="ffn_up_s")
    xs = _mm(act_s, ffn_w2, resid=xs.reshape(bs, d), gate=mods_s[0], gate_col=5, out_dtype=F32,
             tm=bs, tn=1024, tk=2048, name="ffn_down_s").reshape(1, bs, d)

    p_lb = jax.nn.softmax(hg_lower_bounds.astype(F32), axis=0)
    lb = (jnp.cumsum(p_lb, axis=0) - p_lb[0:1])[1].reshape(1, d)

    hp = _prenorm(xp, norm_g4, 2, mods_p[1], 0, tt=256)
    proj_p = _mm(hp.reshape(mp, d), hg_w_in, out_dtype=F32, tm=2048, tn=512, tk=d, name="hg_in").reshape(bp, seq, 4 * d)
    o_p, st_p = _hgrn_prompt(proj_p, lb, tt=256, hb=4)
    og_p = _gated_norm(o_p, proj_p, hg_g_norm, tt=256)
    xp = _mm(og_p.reshape(mp, d), hg_w_o, resid=xp.reshape(mp, d), gate=mods_p[1], gate_col=2,
             gate_group_rows=seq, out_dtype=F32, tm=1024, tn=512, tk=d, name="hg_o").reshape(bp, seq, d)

    hs = _prenorm(xs, norm_g4, 2, mods_s[1], 0, tt=bs)
    proj_s = _mm(hs.reshape(bs, d), hg_w_in, out_dtype=F32, tm=bs, tn=1024, tk=2048, name="hg_in_s")
    o_s, st_s = _hgrn_decode(proj_s.reshape(bs, 4, hg_heads, HG_KDIM), lb.reshape(hg_heads, HG_KDIM), state_hgrn[0])
    og_s = _gated_norm(o_s.reshape(1, bs, d), proj_s.reshape(1, bs, 4 * d), hg_g_norm, tt=bs)
    xs = _mm(og_s.reshape(bs, d), hg_w_o, resid=xs.reshape(bs, d), gate=mods_s[1], gate_col=2, out_dtype=F32,
             tm=bs, tn=1024, tk=2048, name="hg_o_s").reshape(1, bs, d)

    router = jnp.concatenate([moe_router[0], jnp.zeros((d, LANE - N_EXPERTS), F32)], axis=1)
    hp, route_p = _prenorm(xp, norm_g4, 3, mods_p[1], 3, router, tt=256)
    hs, route_s = _prenorm(xs, norm_g4, 3, mods_s[1], 3, router, tt=bs)
    h_all = jnp.concatenate([hp.reshape(mp, d // 2), hs.reshape(bs, d // 2)], axis=0)
    route = jnp.concatenate([route_p.reshape(mp, LANE), route_s.reshape(bs, LANE)], axis=0)
    n_tok = mp + bs
    y_sorted, dest2 = _experts(h_all, route, moe_w1[0], moe_w3[0], moe_w2[0], half=MOE_HALF, tn_up=512,
                               tn_down=2048, tk_down=1024)
    y_tok_p = y_sorted.at[dest2[:, :mp]].get(mode="promise_in_bounds").reshape(TOP_K, bp, seq, d)
    y_tok_s = y_sorted.at[dest2[:, mp:]].get(mode="promise_in_bounds").reshape(TOP_K, 1, bs, d)
    w_tok = jnp.concatenate([route[:, TOP_K:2 * TOP_K], jnp.zeros((n_tok, LANE - TOP_K), F32)], axis=1)
    fn = final_norm.reshape(1, d)
    y_prompt = _moe_combine_final(xp, mods_p[1], 5, y_tok_p, w_tok[:mp].reshape(bp, seq, LANE), fn, tt=256)
    y_sample = _moe_combine_final(xs, mods_s[1], 5, y_tok_s, w_tok[mp:].reshape(1, bs, LANE), fn, tt=bs)

    return (y_prompt, y_sample.reshape(bs, dec_seq, d),
            ckv_p[None], kpe_p[None], st_p[None],
            ckv_s.reshape(1, bs, dec_seq, KV_LORA), kpe_s.reshape(1, bs, dec_seq, ROPE_DIM), st_s[None])


def _experts(h_all, route, w1, w3, w2, *, half, tn_up, tn_down, tk_down):
    n_tok = h_all.shape[0]
    n_pair = n_tok * TOP_K
    tile = 2 * half
    n_tile = pl.cdiv(n_pair + N_EXPERTS * (tile - 1), tile)
    n_half = 2 * n_tile
    r_rows = n_tile * tile
    e_pair = route[:, :TOP_K].astype(I32).reshape(n_pair)
    onehot = (e_pair[:, None] == jnp.arange(N_EXPERTS, dtype=I32)[None, :]).astype(I32)
    counts = jnp.sum(onehot, axis=0)
    rank = jnp.sum(onehot * (jnp.cumsum(onehot, axis=0) - onehot), axis=1)
    padded = ((counts + tile - 1) // tile) * tile
    ends = jnp.cumsum(padded)
    starts = ends - padded
    dest = jnp.sum(onehot * starts[None, :], axis=1) + rank
    src_token = (jnp.arange(r_rows, dtype=I32) % n_tok).at[dest].set(jnp.arange(n_pair, dtype=I32) // TOP_K)
    half_start = jnp.arange(n_half, dtype=I32) * half
    half_e = jnp.minimum(jnp.sum((ends[None, :] <= half_start[:, None]).astype(I32), axis=1), N_EXPERTS - 1)
    half_oh = (half_e[:, None] == jnp.arange(N_EXPERTS, dtype=I32)[None, :]).astype(I32)
    half_occ = half_start < jnp.sum(half_oh * (starts + counts)[None, :], axis=1)

    def occupied_list(occ):
        n = occ.shape[0]
        occ_i = occ.astype(I32)
        n_occ = jnp.sum(occ_i)
        pos = jnp.where(occ, jnp.cumsum(occ_i) - 1, n)
        lst = jnp.zeros((n + 1,), I32).at[pos].set(jnp.arange(n, dtype=I32))[:n]
        return lst[jnp.minimum(jnp.arange(n, dtype=I32), n_occ - 1)], n_occ.reshape(1)

    half_list, n_half_occ = occupied_list(half_occ)
    tile_list, n_tile_occ = occupied_list(half_occ[0::2])
    x_sorted = h_all.at[src_token].get(mode="promise_in_bounds")
    act = _moe_up(half_list, half_e[half_list], n_half_occ, x_sorted, w1, w3, tm=half, tn=tn_up,
                  max_occupied=pl.cdiv(n_pair, half) + N_EXPERTS)
    y_sorted = _moe_down(tile_list, half_e[2 * tile_list], half_occ[2 * tile_list + 1].astype(I32), n_tile_occ,
                         act, w2, tm=tile, tn=tn_down, tk=tk_down)
    return y_sorted, dest.reshape(n_tok, TOP_K).T
```

```python
import functools

import jax
import jax.numpy as jnp
from jax import lax
from jax.experimental import pallas as pl
from jax.experimental.pallas import tpu as pltpu

F32 = jnp.float32
BF16 = jnp.bfloat16
I32 = jnp.int32

PAGE_SIZE = 128
MLA_HEADS = 32
Q_LORA = 1024
KV_LORA = 512
NOPE_DIM = 128
ROPE_DIM = 64
V_DIM = 128
ROPE_THETA = 10000.0
ATTN_SCALE = (NOPE_DIM + ROPE_DIM) ** -0.5
HG_KDIM = 128
HG_CHUNK = 64
N_EXPERTS = 8
TOP_K = 2
N_ADA = 6
EPS = 1e-6
NEG_INF = -1e30

LANE = 128
MIB = 1024 * 1024
VMEM_CAP_BYTES = 56 * MIB


def _params(n_grid, vmem_mib):
    return pltpu.CompilerParams(
        dimension_semantics=("arbitrary",) * n_grid,
        vmem_limit_bytes=min(vmem_mib * MIB, VMEM_CAP_BYTES))


def _silu(x):
    return x * jax.nn.sigmoid(x)


def _split3(x):
    hi = x.astype(BF16)
    r1 = x - hi.astype(F32)
    mid = r1.astype(BF16)
    lo = (r1 - mid.astype(F32)).astype(BF16)
    return hi, mid, lo


def _mm_kernel(*refs, nk, tk, k_valid, a_silu, n_w, epi):
    it = iter(refs)
    a_ref = next(it)
    w_ref = next(it)
    w3_ref = next(it) if n_w == 2 else None
    bias_ref = next(it) if epi == "bias" else None
    x_ref = next(it) if epi == "resid" else None
    gate_ref = next(it) if epi == "resid" else None
    o_ref = next(it)
    acc_ref = next(it) if nk > 1 else None
    acc3_ref = next(it) if (nk > 1 and n_w == 2) else None

    k = pl.program_id(2)
    ragged = (k_valid % tk) != 0
    w_refs = [w_ref, w3_ref][:n_w]

    def dots(masked):
        a = a_ref[...]
        if a_silu:
            a = _silu(a.astype(F32))
        a = a.astype(BF16)
        if masked:
            col = k * tk + lax.broadcasted_iota(I32, a.shape, 1)
            a = jnp.where(col < k_valid, a, jnp.zeros_like(a))
        out = []
        for ref in w_refs:
            w = ref[...]
            if masked:
                row = k * tk + lax.broadcasted_iota(I32, w.shape, 0)
                w = jnp.where(row < k_valid, w, jnp.zeros_like(w))
            out.append(jnp.dot(a, w.astype(BF16), preferred_element_type=F32))
        return out + [None] * (2 - n_w)

    def epilogue(h, h3):
        if epi == "plain":
            o_ref[...] = h.astype(o_ref.dtype)
        elif epi == "bias":
            o_ref[...] = (h + bias_ref[...]).astype(o_ref.dtype)
        elif epi == "swiglu":
            o_ref[...] = (_silu(h) * h3).astype(o_ref.dtype)
        else:
            o_ref[...] = (x_ref[...] + gate_ref[...] * h).astype(o_ref.dtype)

    if nk == 1:
        epilogue(*dots(ragged))
        return

    @pl.when(k == 0)
    def _():
        d, d3 = dots(False)
        acc_ref[...] = d
        if n_w == 2:
            acc3_ref[...] = d3

    if nk > 2:
        @pl.when((k > 0) & (k < nk - 1))
        def _():
            d, d3 = dots(False)
            acc_ref[...] += d
            if n_w == 2:
                acc3_ref[...] += d3

    @pl.when(k == nk - 1)
    def _():
        d, d3 = dots(ragged)
        epilogue(acc_ref[...] + d, acc3_ref[...] + d3 if n_w == 2 else None)


def _mm(a, w, *, layer=0, w3=None, bias=None, resid=None, gate=None, gate_col=0, gate_group_rows=None,
        out_dtype=F32, tm, tn, tk, a_silu=False, name="mm"):
    m, kdim = a.shape
    n = w.shape[-1]
    tm = min(tm, m)
    tn = min(tn, n)
    tk = min(tk, kdim)
    nk = pl.cdiv(kdim, tk)
    grid = (pl.cdiv(m, tm), pl.cdiv(n, tn), nk)
    n_w = 2 if w3 is not None else 1
    epi = "swiglu" if w3 is not None else "bias" if bias is not None else "resid" if resid is not None else "plain"

    a_bufs = 1 if (nk == 1 and grid[1] > 1) else 2
    a_mode = dict(pipeline_mode=pl.Buffered(1)) if a_bufs == 1 else {}
    in_specs = [pl.BlockSpec((tm, tk), lambda i, j, k: (i, k), **a_mode),
                pl.BlockSpec((None, tk, tn), lambda i, j, k: (layer, k, j))]
    args = [a, w]
    if n_w == 2:
        in_specs.append(pl.BlockSpec((None, tk, tn), lambda i, j, k: (layer, k, j)))
        args.append(w3)
    if epi == "bias":
        in_specs.append(pl.BlockSpec((None, 1, tn), lambda i, j, k: (layer, 0, j)))
        args.append(bias)
    if epi == "resid":
        in_specs.append(pl.BlockSpec((tm, tn), lambda i, j, k: (i, j)))
        args.append(resid)
        r = gate.shape[1]
        nj = n // tn
        if r == 1:
            per = gate_group_rows // tm
            in_specs.append(pl.BlockSpec((None, 1, tn), lambda i, j, k: (i // per, 0, gate_col * nj + j)))
        else:
            in_specs.append(pl.BlockSpec((None, tm, tn), lambda i, j, k: (i, 0, gate_col * nj + j)))
        args.append(gate)

    scratch = []
    if nk > 1:
        scratch = [pltpu.VMEM((tm, tn), F32)] * n_w
    out_b = jnp.dtype(out_dtype).itemsize
    est = (a_bufs * tm * tk * a.dtype.itemsize + 2 * n_w * tk * tn * 4 + n_w * tk * tn * 2
           + (n_w + 1) * tm * tn * 4 + 2 * tm * tn * out_b + (3 * tm * tn * 4 if epi == "resid" else 0))
    kern = functools.partial(_mm_kernel, nk=nk, tk=tk, k_valid=kdim, a_silu=a_silu, n_w=n_w, epi=epi)
    return pl.pallas_call(
        kern,
        out_shape=jax.ShapeDtypeStruct((m, n), out_dtype),
        grid=grid,
        in_specs=in_specs,
        out_specs=pl.BlockSpec((tm, tn), lambda i, j, k: (i, j)),
        scratch_shapes=scratch,
        compiler_params=_params(3, est // MIB + 8),
        name=name,
    )(*args)


def _prenorm_kernel(*refs, with_router):
    if with_router:
        x_ref, g_ref, sh_ref, sc_ref, r_ref, o_ref, route_ref = refs
    else:
        x_ref, g_ref, sh_ref, sc_ref, o_ref = refs
    x = x_ref[...]
    y = x * lax.rsqrt(jnp.mean(x * x, axis=-1, keepdims=True) + EPS) * g_ref[...]
    h = y * (1.0 + sc_ref[...]) + sh_ref[...]
    if not with_router:
        o_ref[...] = h.astype(o_ref.dtype)
    else:
        half = h.shape[-1] // 2
        bits = lax.bitcast_convert_type(h.astype(BF16).astype(F32), jnp.uint32)
        o_ref[...] = bits[:, half:] | (bits[:, :half] >> 16)
        h1, h2, _ = _split3(h)
        r = r_ref[...]
        r1, r2, _ = _split3(r)
        logits = (jnp.dot(h1, r1, preferred_element_type=F32)
                  + jnp.dot(h1, r2, preferred_element_type=F32)
                  + jnp.dot(h2, r1, preferred_element_type=F32))
        lane = lax.broadcasted_iota(I32, logits.shape, 1).astype(F32)
        logits = jnp.where(lane < N_EXPERTS, logits, -jnp.inf)
        v1 = jnp.max(logits, axis=-1, keepdims=True)
        i1 = jnp.min(jnp.where(logits == v1, lane, float(LANE)), axis=-1, keepdims=True)
        rest = jnp.where(lane == i1, -jnp.inf, logits)
        v2 = jnp.max(rest, axis=-1, keepdims=True)
        i2 = jnp.min(jnp.where(rest == v2, lane, float(LANE)), axis=-1, keepdims=True)
        e2 = jnp.exp(v2 - v1)
        w1 = 1.0 / (1.0 + e2)
        w2 = e2 / (1.0 + e2)
        route = jnp.where(lane == 0, i1, jnp.where(lane == 1, i2, jnp.where(lane == 2, w1, jnp.where(lane == 3, w2, 0.0))))
        route_ref[...] = route


def _prenorm(x3, norm_g4, g_idx, mod, shift_col, router=None, *, tt):
    b, t, d = x3.shape
    tt = min(tt, t)
    r = mod.shape[1]
    if r == 1:
        mod_spec = lambda c: pl.BlockSpec((None, 1, d), lambda bi, ti: (bi, 0, c))
    else:
        mod_spec = lambda c: pl.BlockSpec((None, tt, d), lambda bi, ti: (bi, ti, c))
    in_specs = [pl.BlockSpec((None, tt, d), lambda bi, ti: (bi, ti, 0)),
                pl.BlockSpec((None, 1, d), lambda bi, ti: (g_idx, 0, 0)),
                mod_spec(shift_col), mod_spec(shift_col + 1)]
    args = [x3, norm_g4, mod, mod]
    if router is None:
        out_shape = [jax.ShapeDtypeStruct((b, t, d), BF16)]
        out_specs = [pl.BlockSpec((None, tt, d), lambda bi, ti: (bi, ti, 0))]
    else:
        out_shape = [jax.ShapeDtypeStruct((b, t, d // 2), jnp.uint32)]
        out_specs = [pl.BlockSpec((None, tt, d // 2), lambda bi, ti: (bi, ti, 0))]
    if router is not None:
        in_specs.append(pl.BlockSpec((d, LANE), lambda bi, ti: (0, 0)))
        args.append(router)
        out_shape.append(jax.ShapeDtypeStruct((b, t, LANE), F32))
        out_specs.append(pl.BlockSpec((None, tt, LANE), lambda bi, ti: (bi, ti, 0)))
    res = pl.pallas_call(
        functools.partial(_prenorm_kernel, with_router=router is not None),
        out_shape=out_shape, grid=(b, t // tt), in_specs=in_specs, out_specs=out_specs,
        compiler_params=_params(2, 48), name="prenorm_router" if router is not None else "prenorm",
    )(*args)
    return res if router is not None else res[0]


def _mla_post_kernel(p_ref, qn_ref, kn_ref, cos_ref, sin_ref, cq_ref, ckv_ref, kpe_ref):
    cq = p_ref[:, :Q_LORA]
    cq = cq * lax.rsqrt(jnp.mean(cq * cq, axis=-1, keepdims=True) + EPS) * qn_ref[...]
    cq_ref[...] = cq.astype(cq_ref.dtype)
    c = p_ref[:, Q_LORA:Q_LORA + KV_LORA]
    ckv_ref[...] = c * lax.rsqrt(jnp.mean(c * c, axis=-1, keepdims=True) + EPS) * kn_ref[...]
    slab = p_ref[:, Q_LORA + KV_LORA:]
    rot = pltpu.roll(slab, ROPE_DIM, 1)
    kpe = slab * cos_ref[...] + rot * sin_ref[...]
    kpe_ref[...] = kpe[:, :ROPE_DIM]


def _mla_post(proj3, q_norm, kv_norm, cos_p, sin_p, *, tt):
    b, t, n = proj3.shape
    tt = min(tt, t)
    return pl.pallas_call(
        _mla_post_kernel,
        out_shape=[jax.ShapeDtypeStruct((b, t, Q_LORA), BF16),
                   jax.ShapeDtypeStruct((b, t, KV_LORA), F32),
                   jax.ShapeDtypeStruct((b, t, ROPE_DIM), F32)],
        grid=(b, t // tt),
        in_specs=[pl.BlockSpec((None, tt, n), lambda bi, ti: (bi, ti, 0)),
                  pl.BlockSpec((1, Q_LORA), lambda bi, ti: (0, 0)),
                  pl.BlockSpec((1, KV_LORA), lambda bi, ti: (0, 0)),
                  pl.BlockSpec((tt, LANE), lambda bi, ti: (ti, 0)),
                  pl.BlockSpec((tt, LANE), lambda bi, ti: (ti, 0))],
        out_specs=[pl.BlockSpec((None, tt, Q_LORA), lambda bi, ti: (bi, ti, 0)),
                   pl.BlockSpec((None, tt, KV_LORA), lambda bi, ti: (bi, ti, 0)),
                   pl.BlockSpec((None, tt, ROPE_DIM), lambda bi, ti: (bi, ti, 0))],
        compiler_params=_params(2, 32), name="mla_post",
    )(proj3, q_norm, kv_norm, cos_p, sin_p)


def _attn_prompt_kernel(q_ref, cos_ref, sin_ref, kt_ref, v_ref, wuk_ref, wuv_ref, o_ref,
                        qs_ref, m_ref, l_ref, acc_ref, *, tq, tkv):
    qi = pl.program_id(1)
    hn = MLA_HEADS * NOPE_DIM
    hr = MLA_HEADS * ROPE_DIM
    n_pair = MLA_HEADS // 2
    cos_t = jnp.concatenate([cos_ref[...]] * (hr // LANE), axis=1)
    sin_t = jnp.concatenate([sin_ref[...]] * (hr // LANE), axis=1)
    qpe = (q_ref[:, hn:hn + hr].astype(F32) * cos_t
           + q_ref[:, hn + hr:hn + 2 * hr].astype(F32) * sin_t).astype(BF16)
    n_full = (qi * tq) // tkv
    rows = n_pair * tq

    for par in range(2):
        for mi in range(n_pair):
            h = 2 * mi + par
            ql = jnp.dot(q_ref[:, h * NOPE_DIM:(h + 1) * NOPE_DIM], wuk_ref[h], preferred_element_type=F32)
            qs_ref[mi * tq:(mi + 1) * tq, :KV_LORA] = ql.astype(BF16)
            qs_ref[mi * tq:(mi + 1) * tq, KV_LORA:] = qpe[:, mi * LANE:(mi + 1) * LANE]
        m_ref[...] = jnp.full(m_ref.shape, -jnp.inf, F32)
        l_ref[...] = jnp.zeros(l_ref.shape, F32)
        acc_ref[...] = jnp.zeros(acc_ref.shape, F32)

        def step(j, masked):
            s = jnp.dot(qs_ref[...], kt_ref[par, j], preferred_element_type=F32) * ATTN_SCALE
            if masked:
                r = lax.broadcasted_iota(I32, (rows, tkv), 0)
                q_pos = qi * tq + (r & (tq - 1))
                k_pos = j * tkv + lax.broadcasted_iota(I32, (rows, tkv), 1)
                s = jnp.where(k_pos <= q_pos, s, NEG_INF)
            m_prev = m_ref[...]
            m_new = jnp.maximum(m_prev, jnp.max(s, axis=1, keepdims=True))
            alpha = jnp.exp(m_prev - m_new)
            p = jnp.exp(s - jnp.tile(m_new, (1, tkv // LANE)))
            l_ref[...] = alpha * l_ref[...] + jnp.sum(p, axis=1, keepdims=True)
            pv = jnp.dot(p.astype(BF16), v_ref[j], preferred_element_type=F32)
            acc_ref[...] = acc_ref[...] * jnp.tile(alpha, (1, KV_LORA // LANE)) + pv
            m_ref[...] = m_new

        def body(j, carry):
            step(j, False)
            return carry

        lax.fori_loop(0, n_full, body, 0)
        step(n_full, True)

        inv = 1.0 / l_ref[...]
        for mi in range(n_pair):
            h = 2 * mi + par
            sl = slice(mi * tq, (mi + 1) * tq)
            lat = (acc_ref[sl, :] * jnp.tile(inv[sl, :], (1, KV_LORA // LANE))).astype(BF16)
            o_ref[:, h * V_DIM:(h + 1) * V_DIM] = jnp.dot(
                lat, wuv_ref[h], preferred_element_type=F32).astype(o_ref.dtype)


def _attn_prompt(q3, cos128, sin128, kt, v4, wuk, wuv, *, tq, tkv):
    b, t, nq = q3.shape
    nkv = t // tkv
    kc = KV_LORA + LANE
    rows = (MLA_HEADS // 2) * tq
    one = pl.Buffered(1)
    return pl.pallas_call(
        functools.partial(_attn_prompt_kernel, tq=tq, tkv=tkv),
        out_shape=jax.ShapeDtypeStruct((b, t, MLA_HEADS * V_DIM), BF16),
        grid=(b, t // tq),
        in_specs=[pl.BlockSpec((None, tq, nq), lambda bi, qi: (bi, qi, 0)),
                  pl.BlockSpec((tq, LANE), lambda bi, qi: (qi, 0)),
                  pl.BlockSpec((tq, LANE), lambda bi, qi: (qi, 0)),
                  pl.BlockSpec((None, 2, nkv, kc, tkv), lambda bi, qi: (bi, 0, 0, 0, 0), pipeline_mode=one),
                  pl.BlockSpec((None, nkv, tkv, KV_LORA), lambda bi, qi: (bi, 0, 0, 0), pipeline_mode=one),
                  pl.BlockSpec((MLA_HEADS, NOPE_DIM, KV_LORA), lambda bi, qi: (0, 0, 0), pipeline_mode=one),
                  pl.BlockSpec((MLA_HEADS, KV_LORA, V_DIM), lambda bi, qi: (0, 0, 0), pipeline_mode=one)],
        out_specs=pl.BlockSpec((None, tq, MLA_HEADS * V_DIM), lambda bi, qi: (bi, qi, 0)),
        scratch_shapes=[pltpu.VMEM((rows, kc), BF16), pltpu.VMEM((rows, LANE), F32),
                        pltpu.VMEM((rows, LANE), F32), pltpu.VMEM((rows, KV_LORA), F32)],
        compiler_params=_params(2, 56), name="mla_attn_prompt",
    )(q3, cos128, sin128, kt, v4, wuk, wuv)


def _head_mm_kernel(a_ref, w_ref, o_ref):
    o_ref[...] = jnp.dot(a_ref[...].astype(BF16), w_ref[...], preferred_element_type=F32).astype(o_ref.dtype)


def _head_mm(a, w, ka, na):
    m = a.shape[0]
    h = w.shape[0]
    return pl.pallas_call(
        _head_mm_kernel,
        out_shape=jax.ShapeDtypeStruct((m, h * na), BF16),
        grid=(h,),
        in_specs=[pl.BlockSpec((m, ka), lambda hi: (0, hi)),
                  pl.BlockSpec((None, ka, na), lambda hi: (hi, 0, 0))],
        out_specs=pl.BlockSpec((m, na), lambda hi: (0, hi)),
        compiler_params=_params(1, 16), name="head_mm",
    )(a, w)


def _rope_q_kernel(x_ref, xr_ref, cos_ref, sin_ref, o_ref):
    o_ref[...] = (x_ref[...].astype(F32) * cos_ref[...] + xr_ref[...].astype(F32) * sin_ref[...]).astype(o_ref.dtype)


def _rope_q(q, cos_t, sin_t):
    m = q.shape[0]
    hr = MLA_HEADS * ROPE_DIM
    nope_blocks = (MLA_HEADS * NOPE_DIM) // hr
    return pl.pallas_call(
        _rope_q_kernel,
        out_shape=jax.ShapeDtypeStruct((m, hr), BF16),
        grid=(1,),
        in_specs=[pl.BlockSpec((m, hr), lambda i: (0, nope_blocks)),
                  pl.BlockSpec((m, hr), lambda i: (0, nope_blocks + 1)),
                  pl.BlockSpec((m, hr), lambda i: (0, 0)),
                  pl.BlockSpec((m, hr), lambda i: (0, 0))],
        out_specs=pl.BlockSpec((m, hr), lambda i: (0, 0)),
        compiler_params=_params(1, 16), name="rope_q",
    )(q, q, cos_t, sin_t)


def _attn_decode_kernel(pt_ref, ql_ref, qp_ref, cn_ref, kn_ref, *rest, ns, pps, n_steps):
    n_pg = ns * pps
    ckv_refs = rest[:n_pg]
    kpe_refs = rest[n_pg:2 * n_pg]
    o_ref = rest[2 * n_pg]
    m_ref, l_ref, acc_ref, kbuf_ref, pbuf_ref = rest[2 * n_pg + 1:]
    c = pl.program_id(1)
    nt = (((1,), (1,)), ((), ()))

    for si in range(ns):
        for p in range(pps):
            kbuf_ref[si, p * PAGE_SIZE:(p + 1) * PAGE_SIZE, :] = ckv_refs[si * pps + p][...].astype(BF16)
            pbuf_ref[si, :, p * PAGE_SIZE:(p + 1) * PAGE_SIZE] = kpe_refs[si * pps + p][...].astype(BF16)

    @pl.when(c == 0)
    def _():
        for si in range(ns):
            cn = cn_ref[si]
            kn = kn_ref[si]
            s_new = (jnp.sum(ql_ref[si].astype(F32) * cn, axis=-1, keepdims=True)
                     + jnp.sum(qp_ref[si].astype(F32) * kn, axis=-1, keepdims=True)) * ATTN_SCALE
            m_ref[si] = jnp.broadcast_to(s_new, m_ref.shape[1:])
            l_ref[si] = jnp.ones(l_ref.shape[1:], F32)
            acc_ref[si] = jnp.broadcast_to(cn, acc_ref.shape[1:])

    for si in range(ns):
        kb = kbuf_ref[si]
        s = (lax.dot_general(ql_ref[si], kb, nt, preferred_element_type=F32)
             + jnp.dot(qp_ref[si], pbuf_ref[si], preferred_element_type=F32)) * ATTN_SCALE
        m_prev = m_ref[si]
        m_new = jnp.maximum(m_prev, jnp.max(s, axis=1, keepdims=True))
        alpha = jnp.exp(m_prev - m_new)
        p_all = jnp.exp(s - jnp.tile(m_new, (1, pps))).astype(BF16)
        l_ref[si] = alpha * l_ref[si] + jnp.sum(p_all.astype(F32), axis=1, keepdims=True)
        pv = jnp.dot(p_all, kb, preferred_element_type=F32)
        acc_ref[si] = acc_ref[si] * jnp.tile(alpha, (1, KV_LORA // LANE)) + pv
        m_ref[si] = m_new

    @pl.when(c == n_steps - 1)
    def _():
        for si in range(ns):
            o_ref[si] = (acc_ref[si] / jnp.tile(l_ref[si], (1, KV_LORA // LANE))).astype(o_ref.dtype)


def _attn_decode(page_table, ql3, qp3, ckv_new3, kpe_new3, cache_ckv, cache_kpe_t, layer, *, ns, pps):
    nb, n_pages = page_table.shape
    n_steps = n_pages // pps
    h = MLA_HEADS

    def page_spec(rows, width, si, p):
        return pl.BlockSpec((None, None, rows, width),
                            lambda bi, ci, pt: (layer, pt[bi * ns + si, ci * pps + p], 0, 0))

    in_specs = [pl.BlockSpec((ns, h, KV_LORA), lambda bi, ci, pt: (bi, 0, 0)),
                pl.BlockSpec((ns, h, ROPE_DIM), lambda bi, ci, pt: (bi, 0, 0)),
                pl.BlockSpec((ns, 1, KV_LORA), lambda bi, ci, pt: (bi, 0, 0)),
                pl.BlockSpec((ns, 1, ROPE_DIM), lambda bi, ci, pt: (bi, 0, 0))]
    in_specs += [page_spec(PAGE_SIZE, KV_LORA, si, p) for si in range(ns) for p in range(pps)]
    in_specs += [page_spec(ROPE_DIM, PAGE_SIZE, si, p) for si in range(ns) for p in range(pps)]
    grid_spec = pltpu.PrefetchScalarGridSpec(
        num_scalar_prefetch=1, grid=(nb // ns, n_steps), in_specs=in_specs,
        out_specs=pl.BlockSpec((ns, h, KV_LORA), lambda bi, ci, pt: (bi, 0, 0)),
        scratch_shapes=[pltpu.VMEM((ns, h, LANE), F32), pltpu.VMEM((ns, h, LANE), F32),
                        pltpu.VMEM((ns, h, KV_LORA), F32),
                        pltpu.VMEM((ns, pps * PAGE_SIZE, KV_LORA), BF16),
                        pltpu.VMEM((ns, ROPE_DIM, pps * PAGE_SIZE), BF16)])
    n_pg = ns * pps
    return pl.pallas_call(
        functools.partial(_attn_decode_kernel, ns=ns, pps=pps, n_steps=n_steps),
        out_shape=jax.ShapeDtypeStruct((nb, h, KV_LORA), BF16),
        grid_spec=grid_spec,
        compiler_params=_params(2, 40), name="mla_attn_decode",
    )(page_table, ql3, qp3, ckv_new3, kpe_new3, *([cache_ckv] * n_pg), *([cache_kpe_t] * n_pg))


def _hgrn_prompt_kernel(q_ref, f_ref, i_ref, lb_ref, o_ref, s_out_ref, s_ref, *, tt, hb, n_tb):
    tb = pl.program_id(2)

    @pl.when(tb == 0)
    def _():
        s_ref[...] = jnp.zeros(s_ref.shape, F32)

    c = HG_CHUNK
    dk = HG_KDIM
    lb = lb_ref[...]
    sg = jax.nn.sigmoid(f_ref[...])
    log_f = jnp.log(lb + (1.0 - lb) * sg)
    kk = (1.0 - lb) * (1.0 - sg)
    qs = _silu(q_ref[...])
    vv = i_ref[...]
    row = lax.broadcasted_iota(I32, (c, c), 0)
    col = lax.broadcasted_iota(I32, (c, c), 1)
    causal = col <= row
    tril = causal.astype(BF16)
    mid = (c - 1) // 2
    nt = (((1,), (1,)), ((), ()))

    for ci in range(tt // c):
        rs = slice(ci * c, (ci + 1) * c)
        lf1, lf2, lf3 = _split3(log_f[rs])
        cum = (jnp.dot(tril, lf1, preferred_element_type=F32)
               + jnp.dot(tril, lf2, preferred_element_type=F32)
               + jnp.dot(tril, lf3, preferred_element_type=F32))
        ref = cum[mid:mid + 1]
        last = cum[c - 1:c]
        qc = qs[rs]
        kc = kk[rs]
        vb = vv[rs].astype(BF16)
        qd = (qc * jnp.exp(cum - ref)).astype(BF16)
        kd = (kc * jnp.exp(ref - cum)).astype(BF16)
        q0 = (qc * jnp.exp(cum)).astype(BF16)
        k2 = kc * jnp.exp(last - cum)
        dec = jnp.exp(last)
        for h in range(hb):
            ls = slice(h * dk, (h + 1) * dk)
            a = lax.dot_general(qd[:, ls], kd[:, ls], nt, preferred_element_type=F32)
            a = jnp.where(causal, a, 0.0).astype(BF16)
            s_prev = s_ref[h]
            o = (jnp.dot(q0[:, ls], s_prev.astype(BF16), preferred_element_type=F32)
                 + jnp.dot(a, vb[:, ls], preferred_element_type=F32))
            o_ref[rs, ls] = o
            dec_col = jnp.broadcast_to(dec[:, ls], (dk, dk)).T
            k2t = k2[:, ls].T.astype(BF16)
            s_ref[h] = dec_col * s_prev + jnp.dot(k2t, vb[:, ls], preferred_element_type=F32)

    @pl.when(tb == n_tb - 1)
    def _():
        s_out_ref[...] = s_ref[...]


def _hgrn_prompt(proj3, lb2, *, tt, hb):
    b, t, n4 = proj3.shape
    d = n4 // 4
    heads = d // HG_KDIM
    w = hb * HG_KDIM
    nhg = heads // hb
    n_tb = t // tt
    return pl.pallas_call(
        functools.partial(_hgrn_prompt_kernel, tt=tt, hb=hb, n_tb=n_tb),
        out_shape=[jax.ShapeDtypeStruct((b, t, d), F32),
                   jax.ShapeDtypeStruct((b, heads, HG_KDIM, HG_KDIM), F32)],
        grid=(b, nhg, n_tb),
        in_specs=[pl.BlockSpec((None, tt, w), lambda bi, hi, ti: (bi, ti, hi)),
                  pl.BlockSpec((None, tt, w), lambda bi, hi, ti: (bi, ti, nhg + hi)),
                  pl.BlockSpec((None, tt, w), lambda bi, hi, ti: (bi, ti, 2 * nhg + hi)),
                  pl.BlockSpec((1, w), lambda bi, hi, ti: (0, hi))],
        out_specs=[pl.BlockSpec((None, tt, w), lambda bi, hi, ti: (bi, ti, hi)),
                   pl.BlockSpec((None, hb, HG_KDIM, HG_KDIM), lambda bi, hi, ti: (bi, hi, 0, 0))],
        scratch_shapes=[pltpu.VMEM((hb, HG_KDIM, HG_KDIM), F32)],
        compiler_params=_params(3, 32), name="hgrn_prompt",
    )(proj3, proj3, proj3, lb2)


def _hgrn_decode_kernel(p_ref, lb_ref, s_ref, o_ref, s_out_ref):
    heads = s_ref.shape[0]
    dk = HG_KDIM
    lb = lb_ref[...]
    sg = jax.nn.sigmoid(p_ref[1])
    fg = lb + (1.0 - lb) * sg
    kk = (1.0 - lb) * (1.0 - sg)
    qs = _silu(p_ref[0])
    vv = p_ref[2]
    outs = []
    for h in range(heads):
        f_col = jnp.broadcast_to(fg[h:h + 1], (dk, dk)).T
        k_col = jnp.broadcast_to(kk[h:h + 1], (dk, dk)).T
        q_col = jnp.broadcast_to(qs[h:h + 1], (dk, dk)).T
        s_new = f_col * s_ref[h] + k_col * vv[h:h + 1]
        s_out_ref[h] = s_new
        outs.append(jnp.sum(q_col * s_new, axis=0, keepdims=True))
    o_ref[...] = jnp.concatenate(outs, axis=0)


def _hgrn_decode(proj4, lb2, state):
    b, _, heads, dk = proj4.shape
    return pl.pallas_call(
        _hgrn_decode_kernel,
        out_shape=[jax.ShapeDtypeStruct((b, heads, dk), F32),
                   jax.ShapeDtypeStruct(state.shape, F32)],
        grid=(b,),
        in_specs=[pl.BlockSpec((None, 4, heads, dk), lambda bi: (bi, 0, 0, 0)),
                  pl.BlockSpec((heads, dk), lambda bi: (0, 0)),
                  pl.BlockSpec((None, heads, dk, dk), lambda bi: (bi, 0, 0, 0))],
        out_specs=[pl.BlockSpec((None, heads, dk), lambda bi: (bi, 0, 0)),
                   pl.BlockSpec((None, heads, dk, dk), lambda bi: (bi, 0, 0, 0))],
        compiler_params=_params(1, 32), name="hgrn_decode",
    )(proj4, lb2, state)


def _gated_norm_kernel(o_ref, g_ref, gn_ref, out_ref):
    o = o_ref[...]
    y = o * lax.rsqrt(jnp.mean(o * o, axis=-1, keepdims=True) + EPS) * gn_ref[...]
    out_ref[...] = (y * _silu(g_ref[...])).astype(out_ref.dtype)


def _gated_norm(o3, proj3, g_norm, *, tt):
    b, t, d = o3.shape
    tt = min(tt, t)
    return pl.pallas_call(
        _gated_norm_kernel,
        out_shape=jax.ShapeDtypeStruct((b, t, d), BF16),
        grid=(b, t // tt),
        in_specs=[pl.BlockSpec((None, tt, d), lambda bi, ti: (bi, ti, 0)),
                  pl.BlockSpec((None, tt, d), lambda bi, ti: (bi, ti, 3)),
                  pl.BlockSpec((1, d), lambda bi, ti: (0, 0))],
        out_specs=pl.BlockSpec((None, tt, d), lambda bi, ti: (bi, ti, 0)),
        compiler_params=_params(2, 48), name="gated_norm",
    )(o3, proj3, g_norm)


def _moe_up_kernel(hl_ref, he_ref, nv_ref, a_ref, w1_ref, w3_ref, o_ref):
    i = pl.program_id(1)

    @pl.when(i < nv_ref[0])
    def _():
        words = a_ref[...]
        half = words.shape[1]
        a_lo = lax.bitcast_convert_type(words << 16, F32).astype(BF16)
        a_hi = lax.bitcast_convert_type(words & jnp.uint32(0xFFFF0000), F32).astype(BF16)

        def proj(w_ref):
            w = w_ref[...].astype(BF16)
            return (jnp.dot(a_lo, w[:half], preferred_element_type=F32)
                    + jnp.dot(a_hi, w[half:], preferred_element_type=F32))

        o_ref[...] = (_silu(proj(w1_ref)) * proj(w3_ref)).astype(o_ref.dtype)


def _moe_up(half_list, half_expert, n_occupied, xs, w1, w3, *, tm, tn, max_occupied):
    r, dh = xs.shape
    d = 2 * dh
    n = w1.shape[-1]
    tn = min(tn, n)
    grid_spec = pltpu.PrefetchScalarGridSpec(
        num_scalar_prefetch=3, grid=(n // tn, min(r // tm, max_occupied)),
        in_specs=[pl.BlockSpec((tm, dh), lambda j, i, hl, he, nv: (hl[i], 0)),
                  pl.BlockSpec((None, d, tn), lambda j, i, hl, he, nv: (he[i], 0, j)),
                  pl.BlockSpec((None, d, tn), lambda j, i, hl, he, nv: (he[i], 0, j))],
        out_specs=pl.BlockSpec((tm, tn), lambda j, i, hl, he, nv: (hl[i], j)))
    est = 2 * tm * dh * 4 + 2 * tm * d * 2 + 4 * d * tn * 4 + 2 * d * tn * 2 + 2 * tm * tn * 2 + 3 * tm * tn * 4
    return pl.pallas_call(
        _moe_up_kernel,
        out_shape=jax.ShapeDtypeStruct((r, n), BF16),
        grid_spec=grid_spec,
        compiler_params=_params(2, est // MIB + 4), name="moe_up",
    )(half_list, half_expert, n_occupied, xs, w1, w3)


def _moe_up_rows_kernel(tl_ref, te_ref, sv_ref, nv_ref, a_ref, w1_ref, w3_ref, o_ref, alo_ref, ahi_ref, *, half):
    i = pl.program_id(0)
    j = pl.program_id(1)
    valid = i < nv_ref[0]

    @pl.when(valid & (j == 0))
    def _():
        words = a_ref[...]
        alo_ref[...] = lax.bitcast_convert_type(words << 16, F32).astype(BF16)
        ahi_ref[...] = lax.bitcast_convert_type(words & jnp.uint32(0xFFFF0000), F32).astype(BF16)

    kh = alo_ref.shape[1]

    def swiglu_rows(rows):
        def proj(w_ref):
            w = w_ref[...].astype(BF16)
            return (jnp.dot(alo_ref[rows, :], w[:kh], preferred_element_type=F32)
                    + jnp.dot(ahi_ref[rows, :], w[kh:], preferred_element_type=F32))
        return (_silu(proj(w1_ref)) * proj(w3_ref)).astype(o_ref.dtype)

    @pl.when(valid & (sv_ref[i] != 0))
    def _():
        o_ref[...] = swiglu_rows(slice(None))

    @pl.when(valid & (sv_ref[i] == 0))
    def _():
        o_ref[:half, :] = swiglu_rows(slice(0, half))
        o_ref[half:, :] = jnp.zeros((o_ref.shape[0] - half, o_ref.shape[1]), o_ref.dtype)


def _moe_up_rows(tile_list, tile_expert, second_valid, n_occupied, xs, w1, w3, *, tm, tn):
    r, dh = xs.shape
    d = 2 * dh
    n = w1.shape[-1]
    tn = min(tn, n)
    nn = n // tn

    def jj(i, j, nv):
        return jnp.where(i < nv[0], j, nn - 1)

    grid_spec = pltpu.PrefetchScalarGridSpec(
        num_scalar_prefetch=4, grid=(r // tm, nn),
        in_specs=[pl.BlockSpec((tm, dh), lambda i, j, tl, te, sv, nv: (tl[i], 0), pipeline_mode=pl.Buffered(1)),
                  pl.BlockSpec((None, d, tn), lambda i, j, tl, te, sv, nv: (te[i], 0, jj(i, j, nv))),
                  pl.BlockSpec((None, d, tn), lambda i, j, tl, te, sv, nv: (te[i], 0, jj(i, j, nv)))],
        out_specs=pl.BlockSpec((tm, tn), lambda i, j, tl, te, sv, nv: (tl[i], jj(i, j, nv))),
        scratch_shapes=[pltpu.VMEM((tm, dh), BF16), pltpu.VMEM((tm, dh), BF16)])
    est = tm * dh * 4 + 2 * tm * dh * 2 + 4 * d * tn * 4 + 2 * d * tn * 2 + 2 * tm * tn * 2 + 3 * tm * tn * 4
    return pl.pallas_call(
        functools.partial(_moe_up_rows_kernel, half=tm // 2),
        out_shape=jax.ShapeDtypeStruct((r, n), BF16),
        grid_spec=grid_spec,
        compiler_params=_params(2, est // MIB + 6), name="moe_up",
    )(tile_list, tile_expert, second_valid, n_occupied, xs, w1, w3)


def _moe_down_kernel(tl_ref, te_ref, sv_ref, nv_ref, a_ref, w_ref, o_ref, *, half):
    i = pl.program_id(0)
    k = pl.program_id(2)
    valid = i < nv_ref[0]
    both = valid & (sv_ref[i] != 0)
    first_only = valid & (sv_ref[i] == 0)

    @pl.when(both & (k == 0))
    def _():
        o_ref[...] = jnp.dot(a_ref[...], w_ref[...].astype(BF16), preferred_element_type=F32)

    @pl.when(both & (k > 0))
    def _():
        o_ref[...] += jnp.dot(a_ref[...], w_ref[...].astype(BF16), preferred_element_type=F32)

    @pl.when(first_only & (k == 0))
    def _():
        o_ref[:half, :] = jnp.dot(a_ref[:half, :], w_ref[...].astype(BF16), preferred_element_type=F32)
        o_ref[half:, :] = jnp.zeros((o_ref.shape[0] - half, o_ref.shape[1]), o_ref.dtype)

    @pl.when(first_only & (k > 0))
    def _():
        o_ref[:half, :] += jnp.dot(a_ref[:half, :], w_ref[...].astype(BF16), preferred_element_type=F32)


def _moe_down(tile_list, tile_expert, second_valid, n_occupied, act, w2, *, tm, tn, tk):
    r, kdim = act.shape
    n = w2.shape[-1]
    tn = min(tn, n)
    tk = min(tk, kdim)
    nk = kdim // tk
    nn = n // tn

    def kk(i, k, nv):
        return jnp.where(i < nv[0], k, nk - 1)

    def jj(i, j, nv):
        return jnp.where(i < nv[0], j, nn - 1)

    grid_spec = pltpu.PrefetchScalarGridSpec(
        num_scalar_prefetch=4, grid=(r // tm, nn, nk),
        in_specs=[pl.BlockSpec((tm, tk), lambda i, j, k, tl, te, sv, nv: (tl[i], kk(i, k, nv))),
                  pl.BlockSpec((None, tk, tn), lambda i, j, k, tl, te, sv, nv: (te[i], kk(i, k, nv), jj(i, j, nv)))],
        out_specs=pl.BlockSpec((tm, tn), lambda i, j, k, tl, te, sv, nv: (tl[i], jj(i, j, nv))))
    est = 2 * tm * tk * 2 + 2 * tk * tn * 4 + tk * tn * 2 + 2 * tm * tn * 4 + tm * tn * 4
    return pl.pallas_call(
        functools.partial(_moe_down_kernel, half=tm // 2),
        out_shape=jax.ShapeDtypeStruct((r, n), F32),
        grid_spec=grid_spec,
        compiler_params=_params(3, est // MIB + 6), name="moe_down",
    )(tile_list, tile_expert, second_valid, n_occupied, act, w2)


def _moe_combine_kernel(x_ref, gate_ref, y_ref, w_ref, fn_ref, o_ref):
    w = w_ref[...]
    y = w[:, 0:1] * y_ref[0] + w[:, 1:2] * y_ref[1]
    x = x_ref[...] + gate_ref[...] * y
    o_ref[...] = x * lax.rsqrt(jnp.mean(x * x, axis=-1, keepdims=True) + EPS) * fn_ref[...]


def _moe_combine_final(x3, mod, gate_col, y4, w3, final_norm, *, tt):
    b, t, d = x3.shape
    tt = min(tt, t)
    r = mod.shape[1]
    if r == 1:
        gate_spec = pl.BlockSpec((None, 1, d), lambda bi, ti: (bi, 0, gate_col))
    else:
        gate_spec = pl.BlockSpec((None, tt, d), lambda bi, ti: (bi, ti, gate_col))
    return pl.pallas_call(
        _moe_combine_kernel,
        out_shape=jax.ShapeDtypeStruct((b, t, d), F32),
        grid=(b, t // tt),
        in_specs=[pl.BlockSpec((None, tt, d), lambda bi, ti: (bi, ti, 0)),
                  gate_spec,
                  pl.BlockSpec((TOP_K, None, tt, d), lambda bi, ti: (0, bi, ti, 0)),
                  pl.BlockSpec((None, tt, LANE), lambda bi, ti: (bi, ti, 0)),
                  pl.BlockSpec((1, d), lambda bi, ti: (0, 0))],
        out_specs=pl.BlockSpec((None, tt, d), lambda bi, ti: (bi, ti, 0)),
        compiler_params=_params(2, 48), name="moe_combine_final",
    )(x3, mod, y4, w3, final_norm)


MOE_HALF = 512


def kernel(x_prompt, x_sample, cache_ckv, cache_kpe, state_hgrn, page_table, c_prompt, c_sample,
           ada_w, ada_b, norm_g, mla_w_in, mla_q_norm, mla_w_uq, mla_kv_norm, mla_w_uk, mla_w_uv,
           mla_w_o, hg_w_in, hg_lower_bounds, hg_g_norm, hg_w_o, ffn_w1, ffn_w3, ffn_w2,
           moe_router, moe_w1, moe_w3, moe_w2, final_norm):
    bp, seq, d = x_prompt.shape
    bs, dec_seq, _ = x_sample.shape
    depth = ada_w.shape[0]
    assert depth == 2 and dec_seq == 1
    mp = bp * seq
    past_len = page_table.shape[1] * PAGE_SIZE
    heads = MLA_HEADS
    hg_heads = d // HG_KDIM

    n_c = bp + bs
    pad_c = (-n_c) % 8
    c_all = jnp.concatenate([c_prompt, c_sample, jnp.zeros((pad_c, d), F32)], axis=0)
    ada_b3 = ada_b.reshape(depth, 1, N_ADA * d)
    mods_p, mods_s = [], []
    for i in range(depth):
        mod = _mm(c_all, ada_w, layer=i, bias=ada_b3, out_dtype=F32, tm=n_c + pad_c, tn=1024, tk=2048,
                  a_silu=True, name="ada_mod")
        mods_p.append(mod[:bp].reshape(bp, 1, N_ADA * d))
        mods_s.append(mod[bp:bp + bs].reshape(1, bs, N_ADA * d))

    norm_g4 = norm_g.reshape(depth * 2, 1, d)
    xp = x_prompt
    xs = x_sample.reshape(1, bs, d)

    inv = ROPE_THETA ** (-jnp.arange(0, ROPE_DIM, 2, dtype=F32) / ROPE_DIM)

    def tables(pos):
        ang = pos.astype(F32)[:, None] * inv[None, :]
        cos2 = jnp.concatenate([jnp.cos(ang)] * 2, axis=1)
        sin2 = jnp.concatenate([jnp.sin(ang)] * 2, axis=1)
        return cos2, sin2

    cos_p, sin_p = tables(jnp.arange(seq))
    cos_s, sin_s = tables(jnp.broadcast_to(past_len + jnp.arange(dec_seq), (bs,)))
    zpad = lambda a: jnp.concatenate([a, jnp.zeros_like(a)], axis=1)
    dup = lambda a: jnp.concatenate([a, a], axis=1)

    w_in = mla_w_in[0]
    kpe_w = w_in[:, Q_LORA + KV_LORA:]
    half = ROPE_DIM // 2
    w_in_aug = jnp.concatenate([w_in, -kpe_w[:, half:], kpe_w[:, :half]], axis=1)[None]
    wq = mla_w_uq[0].reshape(Q_LORA, heads, NOPE_DIM + ROPE_DIM)
    wq_rope = wq[:, :, NOPE_DIM:]
    wq_rot = jnp.concatenate([-wq_rope[..., half:], wq_rope[..., :half]], axis=-1)
    w_uq_aug = jnp.concatenate([wq[:, :, :NOPE_DIM].reshape(Q_LORA, -1), wq_rope.reshape(Q_LORA, -1),
                                wq_rot.reshape(Q_LORA, -1)], axis=1)[None]
    wuk = jnp.transpose(mla_w_uk[0], (1, 2, 0)).astype(BF16)
    wuv = jnp.transpose(mla_w_uv[0], (1, 0, 2)).astype(BF16)
    q_norm = mla_q_norm[0].reshape(1, Q_LORA)
    kv_norm = mla_kv_norm[0].reshape(1, KV_LORA)

    hp = _prenorm(xp, norm_g4, 0, mods_p[0], 0, tt=256)
    proj_p = _mm(hp.reshape(mp, d), w_in_aug, out_dtype=F32, tm=1024, tn=2048, tk=1024, name="mla_in")
    cq_p, ckv_p, kpe_p = _mla_post(proj_p.reshape(bp, seq, -1), q_norm, kv_norm, zpad(cos_p), zpad(sin_p), tt=512)
    q_p = _mm(cq_p.reshape(mp, Q_LORA), w_uq_aug, out_dtype=BF16, tm=2048, tn=1024, tk=1024, name="mla_uq")
    tkv = 512
    nkv = seq // tkv
    ckv_b = ckv_p.astype(BF16)
    kpe_b = kpe_p.astype(BF16)
    zk = jnp.zeros_like(kpe_b)
    kcat = jnp.stack([jnp.concatenate([ckv_b, kpe_b, zk], axis=-1),
                      jnp.concatenate([ckv_b, zk, kpe_b], axis=-1)], axis=1)
    kt = jnp.swapaxes(kcat.reshape(bp, 2, nkv, tkv, KV_LORA + LANE), -1, -2)
    v4 = ckv_b.reshape(bp, nkv, tkv, KV_LORA)
    o_p = _attn_prompt(q_p.reshape(bp, seq, -1), dup(cos_p), dup(sin_p), kt, v4, wuk, wuv, tq=128, tkv=tkv)
    xp = _mm(o_p.reshape(mp, d), mla_w_o, resid=xp.reshape(mp, d), gate=mods_p[0], gate_col=2,
             gate_group_rows=seq, out_dtype=F32, tm=1024, tn=512, tk=d, name="mla_o").reshape(bp, seq, d)

    hs = _prenorm(xs, norm_g4, 0, mods_s[0], 0, tt=bs)
    proj_s = _mm(hs.reshape(bs, d), w_in_aug, out_dtype=F32, tm=bs, tn=2048, tk=2048, name="mla_in_s")
    cq_s, ckv_s, kpe_s = _mla_post(proj_s.reshape(1, bs, -1), q_norm, kv_norm, zpad(cos_s), zpad(sin_s), tt=bs)
    q_s = _mm(cq_s.reshape(bs, Q_LORA), w_uq_aug, out_dtype=BF16, tm=bs, tn=2048, tk=1024, name="mla_uq_s")
    ql_s = _head_mm(q_s, wuk, NOPE_DIM, KV_LORA).reshape(bs, heads, KV_LORA)
    qp_s = _rope_q(q_s, jnp.tile(cos_s, (1, heads)), jnp.tile(sin_s, (1, heads))).reshape(bs, heads, ROPE_DIM)
    lat_s = _attn_decode(page_table, ql_s, qp_s, ckv_s.reshape(bs, 1, KV_LORA), kpe_s.reshape(bs, 1, ROPE_DIM),
                         cache_ckv, jnp.swapaxes(cache_kpe, 2, 3), 0, ns=1, pps=32)
    o_s = _head_mm(lat_s.reshape(bs, heads * KV_LORA), wuv, KV_LORA, V_DIM)
    xs = _mm(o_s, mla_w_o, resid=xs.reshape(bs, d), gate=mods_s[0], gate_col=2, out_dtype=F32,
             tm=bs, tn=1024, tk=2048, name="mla_o_s").reshape(1, bs, d)

    hp = _prenorm(xp, norm_g4, 1, mods_p[0], 3, tt=256)
    act_p = _mm(hp.reshape(mp, d), ffn_w1, w3=ffn_w3, out_dtype=BF16, tm=2048, tn=256, tk=d, name="ffn_up")
    xp = _mm(act_p, ffn_w2, resid=xp.reshape(mp, d), gate=mods_p[0], gate_col=5, gate_group_rows=seq,
             out_dtype=F32, tm=1024, tn=1024, tk=1024, name="ffn_down").reshape(bp, seq, d)
    hs = _prenorm(xs, norm_g4, 1, mods_s[0], 3, tt=bs)
    act_s = _mm(hs.reshape(bs, d), ffn_w1, w3=ffn_w3, out_dtype=BF16, tm=bs, tn=1024, tk=1024, name="ffn_up_s")
    xs = _mm(act_s, ffn_w2, resid=xs.reshape(bs, d), gate=mods_s[0], gate_col=5, out_dtype=F32,
             tm=bs, tn=1024, tk=2048, name="ffn_down_s").reshape(1, bs, d)

    p_lb = jax.nn.softmax(hg_lower_bounds.astype(F32), axis=0)
    lb = (jnp.cumsum(p_lb, axis=0) - p_lb[0:1])[1].reshape(1, d)

    hp = _prenorm(xp, norm_g4, 2, mods_p[1], 0, tt=256)
    proj_p = _mm(hp.reshape(mp, d), hg_w_in, out_dtype=F32, tm=2048, tn=512, tk=d, name="hg_in").reshape(bp, seq, 4 * d)
    o_p, st_p = _hgrn_prompt(proj_p, lb, tt=256, hb=4)
    og_p = _gated_norm(o_p, proj_p, hg_g_norm, tt=256)
    xp = _mm(og_p.reshape(mp, d), hg_w_o, resid=xp.reshape(mp, d), gate=mods_p[1], gate_col=2,
             gate_group_rows=seq, out_dtype=F32, tm=1024, tn=512, tk=d, name="hg_o").reshape(bp, seq, d)

    hs = _prenorm(xs, norm_g4, 2, mods_s[1], 0, tt=bs)
    proj_s = _mm(hs.reshape(bs, d), hg_w_in, out_dtype=F32, tm=bs, tn=1024, tk=2048, name="hg_in_s")
    o_s, st_s = _hgrn_decode(proj_s.reshape(bs, 4, hg_heads, HG_KDIM), lb.reshape(hg_heads, HG_KDIM), state_hgrn[0])
    og_s = _gated_norm(o_s.reshape(1, bs, d), proj_s.reshape(1, bs, 4 * d), hg_g_norm, tt=bs)
    xs = _mm(og_s.reshape(bs, d), hg_w_o, resid=xs.reshape(bs, d), gate=mods_s[1], gate_col=2, out_dtype=F32,
             tm=bs, tn=1024, tk=2048, name="hg_o_s").reshape(1, bs, d)

    router = jnp.concatenate([moe_router[0], jnp.zeros((d, LANE - N_EXPERTS), F32)], axis=1)
    hp, route_p = _prenorm(xp, norm_g4, 3, mods_p[1], 3, router, tt=256)
    hs, route_s = _prenorm(xs, norm_g4, 3, mods_s[1], 3, router, tt=bs)
    h_all = jnp.concatenate([hp.reshape(mp, d // 2), hs.reshape(bs, d // 2)], axis=0)
    route = jnp.concatenate([route_p.reshape(mp, LANE), route_s.reshape(bs, LANE)], axis=0)
    n_tok = mp + bs
    y_sorted, dest2 = _experts(h_all, route, moe_w1[0], moe_w3[0], moe_w2[0], half=MOE_HALF, tn_up=256,
                               tn_down=2048, tk_down=1024)
    y_tok_p = y_sorted.at[dest2[:, :mp]].get(mode="promise_in_bounds").reshape(TOP_K, bp, seq, d)
    y_tok_s = y_sorted.at[dest2[:, mp:]].get(mode="promise_in_bounds").reshape(TOP_K, 1, bs, d)
    w_tok = jnp.concatenate([route[:, TOP_K:2 * TOP_K], jnp.zeros((n_tok, LANE - TOP_K), F32)], axis=1)
    fn = final_norm.reshape(1, d)
    y_prompt = _moe_combine_final(xp, mods_p[1], 5, y_tok_p, w_tok[:mp].reshape(bp, seq, LANE), fn, tt=256)
    y_sample = _moe_combine_final(xs, mods_s[1], 5, y_tok_s, w_tok[mp:].reshape(1, bs, LANE), fn, tt=bs)

    return (y_prompt, y_sample.reshape(bs, dec_seq, d),
            ckv_p[None], kpe_p[None], st_p[None],
            ckv_s.reshape(1, bs, dec_seq, KV_LORA), kpe_s.reshape(1, bs, dec_seq, ROPE_DIM), st_s[None])


def _experts(h_all, route, w1, w3, w2, *, half, tn_up, tn_down, tk_down):
    n_tok = h_all.shape[0]
    n_pair = n_tok * TOP_K
    tile = 2 * half
    n_tile = pl.cdiv(n_pair + N_EXPERTS * (tile - 1), tile)
    n_half = 2 * n_tile
    r_rows = n_tile * tile
    e_pair = route[:, :TOP_K].astype(I32).reshape(n_pair)
    onehot = (e_pair[:, None] == jnp.arange(N_EXPERTS, dtype=I32)[None, :]).astype(I32)
    counts = jnp.sum(onehot, axis=0)
    rank = jnp.sum(onehot * (jnp.cumsum(onehot, axis=0) - onehot), axis=1)
    padded = ((counts + tile - 1) // tile) * tile
    ends = jnp.cumsum(padded)
    starts = ends - padded
    dest = jnp.sum(onehot * starts[None, :], axis=1) + rank
    src_token = (jnp.arange(r_rows, dtype=I32) % n_tok).at[dest].set(jnp.arange(n_pair, dtype=I32) // TOP_K)
    half_start = jnp.arange(n_half, dtype=I32) * half
    half_e = jnp.minimum(jnp.sum((ends[None, :] <= half_start[:, None]).astype(I32), axis=1), N_EXPERTS - 1)
    half_oh = (half_e[:, None] == jnp.arange(N_EXPERTS, dtype=I32)[None, :]).astype(I32)
    half_occ = half_start < jnp.sum(half_oh * (starts + counts)[None, :], axis=1)

    def occupied_list(occ):
        n = occ.shape[0]
        occ_i = occ.astype(I32)
        n_occ = jnp.sum(occ_i)
        pos = jnp.where(occ, jnp.cumsum(occ_i) - 1, n)
        lst = jnp.zeros((n + 1,), I32).at[pos].set(jnp.arange(n, dtype=I32))[:n]
        return lst[jnp.minimum(jnp.arange(n, dtype=I32), n_occ - 1)], n_occ.reshape(1)

    tile_list, n_tile_occ = occupied_list(half_occ[0::2])
    x_sorted = h_all.at[src_token].get(mode="promise_in_bounds")
    tile_e = half_e[2 * tile_list]
    tile_second = half_occ[2 * tile_list + 1].astype(I32)
    act = _moe_up_rows(tile_list, tile_e, tile_second, n_tile_occ, x_sorted, w1, w3, tm=tile, tn=tn_up)
    y_sorted = _moe_down(tile_list, tile_e, tile_second, n_tile_occ, act, w2, tm=tile, tn=tn_down, tk=tk_down)
    return y_sorted, dest.reshape(n_tok, TOP_K).T
```
